```python
import math
import jax, jax.numpy as jnp
from jax import lax
import numpy as np

D_MODEL = 1024
BATCH = 8
SEQ = 8192
DEPTH = 2

CTX_LEN = 256
GRID_W = 64
N_MIXERS = 2
N_LAYERS_A = (DEPTH + N_MIXERS - 1) // N_MIXERS
N_LAYERS_B = DEPTH // N_MIXERS
EPS = 1e-6

GM_WIDTH = 2 * D_MODEL
GM_GROUPS = 8
GM_GROUP_DIM = GM_WIDTH // GM_GROUPS
CHUNK = 128

DA_HEADS = D_MODEL // 128
DA_HEAD_DIM = 64
DA_WIDTH = DA_HEADS * 2 * DA_HEAD_DIM
Q_BLOCK = 128
ROPE_THETA = 10000.0

FFN_DIM = 2816
CONV_W = 3

kernel_name = 'hybrid_gmlp_diffattn_dit_block'


def _rms(x, eps=EPS):
    xf = x.astype(jnp.float32)
    return (xf * lax.rsqrt(jnp.mean(xf * xf, axis=-1, keepdims=True) + eps)).astype(x.dtype)


def _modulate(x, shift, scale):
    return _rms(x) * (1 + scale) + shift


def _ada_params(cond, w, b):
    m = jnp.dot(jax.nn.silu(cond), w) + b
    return jnp.split(m[..., None, :], 6, axis=-1)


def _chunk_gmlp(h, w_in, norm_g, w_s, b_s, w_out):
    bsz, length, _ = h.shape
    u, v = jnp.split(jax.nn.gelu(h @ w_in), 2, axis=-1)
    v = (_rms(v) * norm_g).reshape(bsz, length // CHUNK, CHUNK, GM_GROUPS, GM_GROUP_DIM)
    v = jnp.einsum('gij,bcjgd->bcigd', w_s, v) + b_s.T[None, None, :, :, None]
    return (u * v.reshape(bsz, length, GM_WIDTH)) @ w_out


def _axial_rope_tables(rows):
    row_pos = jnp.broadcast_to(jnp.arange(rows, dtype=jnp.float32)[:, None], (rows, GRID_W)).reshape(-1)
    col_pos = jnp.broadcast_to(jnp.arange(GRID_W, dtype=jnp.float32)[None, :], (rows, GRID_W)).reshape(-1)
    n_freq = DA_HEAD_DIM // 4
    inv_freq = ROPE_THETA ** (-jnp.arange(n_freq, dtype=jnp.float32) / n_freq)
    ang_r = row_pos[:, None] * inv_freq
    ang_c = col_pos[:, None] * inv_freq
    ang = jnp.concatenate([ang_r, ang_r, ang_c, ang_c], axis=-1)
    return jnp.cos(ang), jnp.sin(ang)


def _apply_rope(x, cos, sin):
    xr = x.reshape(*x.shape[:-1], 2, 2, DA_HEAD_DIM // 4)
    rot = jnp.concatenate([-xr[..., 1:, :], xr[..., :1, :]], axis=-2).reshape(x.shape)
    return x * cos[:, None, :].astype(x.dtype) + rot * sin[:, None, :].astype(x.dtype)


def _diff_attend(q, k, v, lam):
    s = jnp.einsum('bqhmd,bkhmd->bhmqk', q * (DA_HEAD_DIM ** -0.5), k, preferred_element_type=jnp.float32)
    p = jax.nn.softmax(s, axis=-1)
    a = p[:, :, 0] - lam * p[:, :, 1]
    return jnp.einsum('bhqk,bkhe->bqhe', a.astype(v.dtype), v, preferred_element_type=jnp.float32)


def _diff_head_out(o, subln_g, w_out, lambda_init, dtype):
    o = _rms(o) * subln_g * (1.0 - lambda_init)
    return o.reshape(*o.shape[:2], DA_WIDTH).astype(dtype) @ w_out


def _diff_attention(h_lat, h_ctx, w_qkv, lq1, lk1, lq2, lk2, subln_g, w_out, cos, sin, lambda_init, need_ctx_out):
    bsz, n_lat, _ = h_lat.shape
    n_ctx = h_ctx.shape[1]
    lam = jnp.exp(jnp.sum(lq1 * lk1)) - jnp.exp(jnp.sum(lq2 * lk2)) + lambda_init
    q_l, k_l, v_l = jnp.split(h_lat @ w_qkv, 3, axis=-1)
    q_l = _apply_rope(q_l.reshape(bsz, n_lat, 2 * DA_HEADS, DA_HEAD_DIM), cos, sin)
    k_l = _apply_rope(k_l.reshape(bsz, n_lat, 2 * DA_HEADS, DA_HEAD_DIM), cos, sin)
    k_c, v_c = jnp.split(h_ctx @ w_qkv[:, DA_WIDTH:], 2, axis=-1)
    k_c = k_c.reshape(bsz, n_ctx, DA_HEADS, 2, DA_HEAD_DIM)
    v_c = v_c.reshape(bsz, n_ctx, DA_HEADS, 2 * DA_HEAD_DIM)
    k_all = jnp.concatenate([k_c, k_l.reshape(bsz, n_lat, DA_HEADS, 2, DA_HEAD_DIM)], axis=1)
    v_all = jnp.concatenate([v_c, v_l.reshape(bsz, n_lat, DA_HEADS, 2 * DA_HEAD_DIM)], axis=1)
    n_blk = n_lat // Q_BLOCK
    q_blocks = q_l.reshape(bsz, n_blk, Q_BLOCK, DA_HEADS, 2, DA_HEAD_DIM).transpose(1, 0, 2, 3, 4, 5)
    o_blocks = lax.map(lambda qb: _diff_attend(qb, k_all, v_all, lam), q_blocks)
    o_lat = o_blocks.transpose(1, 0, 2, 3, 4).reshape(bsz, n_lat, DA_HEADS, 2 * DA_HEAD_DIM)
    out_lat = _diff_head_out(o_lat, subln_g, w_out, lambda_init, h_lat.dtype)
    if not need_ctx_out:
        return out_lat, None
    q_c = (h_ctx @ w_qkv[:, :DA_WIDTH]).reshape(bsz, n_ctx, DA_HEADS, 2, DA_HEAD_DIM)
    o_ctx = _diff_attend(q_c, k_c, v_c, lam)
    return out_lat, _diff_head_out(o_ctx, subln_g, w_out, lambda_init, h_ctx.dtype)


def _conv_ffn(h, w_up, conv_w, conv_b, w_down):
    z = h @ w_up
    length = z.shape[1]
    pad = CONV_W // 2
    zp = jnp.pad(z, ((0, 0), (pad, CONV_W - 1 - pad), (0, 0)))
    z = sum(zp[:, j:j + length] * conv_w[j] for j in range(CONV_W)) + conv_b
    a, g = jnp.split(z, 2, axis=-1)
    return (jax.nn.silu(a) * g) @ w_down


def setup_inputs(seed: int = 0) -> dict:
    key = jax.random.key(seed)
    ks = jax.random.split(key, 24)
    nrm = jax.random.normal
    f32 = jnp.float32
    return {
        'x': nrm(ks[0], (BATCH, SEQ, D_MODEL), f32),
        'c': nrm(ks[1], (BATCH, D_MODEL), f32),
        'ctx': nrm(ks[2], (BATCH, CTX_LEN, D_MODEL), f32),
        'c_ctx': nrm(ks[3], (D_MODEL,), f32),
        'ada_w': nrm(ks[4], (DEPTH, D_MODEL, 6 * D_MODEL), f32) * D_MODEL ** -0.5,
        'ada_b': nrm(ks[5], (DEPTH, 6 * D_MODEL), f32) * 0.02,
        'gm_w_in': nrm(ks[6], (N_LAYERS_A, D_MODEL, 2 * GM_WIDTH), f32) * D_MODEL ** -0.5,
        'gm_norm_g': 1.0 + 0.02 * nrm(ks[7], (N_LAYERS_A, GM_WIDTH), f32),
        'gm_w_s': nrm(ks[8], (N_LAYERS_A, GM_GROUPS, CHUNK, CHUNK), f32) * CHUNK ** -0.5,
        'gm_b_s': 1.0 + 0.02 * nrm(ks[9], (N_LAYERS_A, GM_GROUPS, CHUNK), f32),
        'gm_w_out': nrm(ks[10], (N_LAYERS_A, GM_WIDTH, D_MODEL), f32) * GM_WIDTH ** -0.5,
        'da_w_qkv': nrm(ks[11], (N_LAYERS_B, D_MODEL, 3 * DA_WIDTH), f32) * D_MODEL ** -0.5,
        'da_lambda_q1': 0.1 * nrm(ks[12], (N_LAYERS_B, DA_HEAD_DIM), f32),
        'da_lambda_k1': 0.1 * nrm(ks[13], (N_LAYERS_B, DA_HEAD_DIM), f32),
        'da_lambda_q2': 0.1 * nrm(ks[14], (N_LAYERS_B, DA_HEAD_DIM), f32),
        'da_lambda_k2': 0.1 * nrm(ks[15], (N_LAYERS_B, DA_HEAD_DIM), f32),
        'da_subln_g': 1.0 + 0.02 * nrm(ks[16], (N_LAYERS_B, 2 * DA_HEAD_DIM), f32),
        'da_w_out': nrm(ks[17], (N_LAYERS_B, DA_WIDTH, D_MODEL), f32) * DA_WIDTH ** -0.5,
        'ffn_w_up': nrm(ks[18], (DEPTH, D_MODEL, 2 * FFN_DIM), f32) * D_MODEL ** -0.5,
        'ffn_conv_w': nrm(ks[19], (DEPTH, CONV_W, 2 * FFN_DIM), f32) * CONV_W ** -0.5,
        'ffn_conv_b': 0.02 * nrm(ks[20], (DEPTH, 2 * FFN_DIM), f32),
        'ffn_w_down': nrm(ks[21], (DEPTH, FFN_DIM, D_MODEL), f32) * FFN_DIM ** -0.5,
        'final_norm_g': 1.0 + 0.02 * nrm(ks[22], (D_MODEL,), f32),
    }


def reference(x, c, ctx, c_ctx, ada_w, ada_b, gm_w_in, gm_norm_g, gm_w_s, gm_b_s, gm_w_out,
              da_w_qkv, da_lambda_q1, da_lambda_k1, da_lambda_q2, da_lambda_k2, da_subln_g, da_w_out,
              ffn_w_up, ffn_conv_w, ffn_conv_b, ffn_w_down, final_norm_g):
    n_lat = x.shape[1]
    ROWS = n_lat // GRID_W
    cos, sin = _axial_rope_tables(ROWS)
    h, hc = x, ctx
    for i in range(DEPTH):
        j = i // N_MIXERS
        last = i == DEPTH - 1
        sh1, sc1, g1, sh2, sc2, g2 = _ada_params(c, ada_w[i], ada_b[i])
        csh1, csc1, cg1, csh2, csc2, cg2 = _ada_params(c_ctx, ada_w[i], ada_b[i])
        xl = _modulate(h, sh1, sc1)
        if i % N_MIXERS == 0:
            m_lat = _chunk_gmlp(xl, gm_w_in[j], gm_norm_g[j], gm_w_s[j], gm_b_s[j], gm_w_out[j])
            m_ctx = None if last else _chunk_gmlp(_modulate(hc, csh1, csc1), gm_w_in[j], gm_norm_g[j],
                                                  gm_w_s[j], gm_b_s[j], gm_w_out[j])
        else:
            lambda_init = 0.8 - 0.6 * math.exp(-0.3 * i)
            m_lat, m_ctx = _diff_attention(xl, _modulate(hc, csh1, csc1), da_w_qkv[j],
                                           da_lambda_q1[j], da_lambda_k1[j], da_lambda_q2[j], da_lambda_k2[j],
                                           da_subln_g[j], da_w_out[j], cos, sin, lambda_init, not last)
        h = h + g1 * m_lat
        h = h + g2 * _conv_ffn(_modulate(h, sh2, sc2), ffn_w_up[i], ffn_conv_w[i], ffn_conv_b[i], ffn_w_down[i])
        if not last:
            hc = hc + cg1 * m_ctx
            hc = hc + cg2 * _conv_ffn(_modulate(hc, csh2, csc2), ffn_w_up[i], ffn_conv_w[i], ffn_conv_b[i],
                                      ffn_w_down[i])
    return _rms(h) * final_norm_g
```

```python
import functools
import math

import jax
import jax.numpy as jnp
from jax import lax
from jax.experimental import pallas as pl
from jax.experimental.pallas import tpu as pltpu

EPS = 1e-6
GRID_W = 64
GM_GROUPS = 8
CHUNK = 128
SUB_HEAD = 64
HEAD_W = 2 * SUB_HEAD
ROPE_THETA = 10000.0
CONV_W = 3
HALO = 16
LANES = 128
ADA_ROWS = 16
VMEM_LIMIT_BYTES = 56 * 1024 * 1024

F32 = jnp.float32
BF16 = jnp.bfloat16


def _rms(x):
    return x * lax.rsqrt(jnp.mean(x * x, axis=-1, keepdims=True) + EPS)


def _modulate(x, mod_ref, k):
    return _rms(x) * (1.0 + mod_ref[k + 1:k + 2, :]) + mod_ref[k:k + 1, :]


def _silu(x):
    return x * (1.0 / (1.0 + jnp.exp(-x)))


def _gelu_tanh(x):
    return 0.5 * x * (1.0 + jnp.tanh(math.sqrt(2.0 / math.pi) * (x + 0.044715 * (x * x * x))))


def _params(n_axes):
    return pltpu.CompilerParams(dimension_semantics=("arbitrary",) * n_axes,
                                vmem_limit_bytes=VMEM_LIMIT_BYTES)


def _resident(shape):
    nd = len(shape)
    return pl.BlockSpec(shape, lambda *_: (0,) * nd, pipeline_mode=pl.Buffered(1))


def _ada_kernel(cond_ref, w_ref, b_ref, out_ref):
    s = _silu(cond_ref[...])
    out_ref[...] = jnp.dot(s, w_ref[...], preferred_element_type=F32,
                           precision=lax.Precision.HIGHEST) + b_ref[...]


def _ada(cond, ada_w, ada_b):
    depth, d, n = ada_w.shape
    nb = 6 * LANES * 2
    return pl.pallas_call(
        _ada_kernel,
        grid=(depth, n // nb),
        in_specs=[pl.BlockSpec((ADA_ROWS, d), lambda l, j: (0, 0)),
                  pl.BlockSpec((None, d, nb), lambda l, j: (l, 0, j)),
                  pl.BlockSpec((None, 1, nb), lambda l, j: (l, 0, j))],
        out_specs=pl.BlockSpec((None, ADA_ROWS, nb), lambda l, j: (l, 0, j)),
        out_shape=jax.ShapeDtypeStruct((depth, ADA_ROWS, n), F32),
        compiler_params=_params(2),
        name="ada",
    )(cond, ada_w, ada_b.reshape(depth, 1, n))


def _mod_spec(d, ctx_row):
    if ctx_row is None:
        return pl.BlockSpec((None, 6, d), lambda b, i: (b, 0, 0))
    return pl.BlockSpec((None, 6, d), lambda b, i: (ctx_row, 0, 0))


def _gmlp_kernel(h_ref, mod_ref, win_ref, ng_ref, ws_ref, bs_ref, wout_ref, out_ref, vn_ref, uv_ref, *, tm, gw):
    x = h_ref[...]
    xl = _modulate(x, mod_ref, 0).astype(BF16)
    v = _gelu_tanh(jnp.dot(xl, win_ref[:, gw:], preferred_element_type=F32))
    vn_ref[...] = (_rms(v) * ng_ref[...]).astype(BF16)
    gd = gw // GM_GROUPS
    for g in range(GM_GROUPS):
        cols = slice(g * gd, (g + 1) * gd)
        u = _gelu_tanh(jnp.dot(xl, win_ref[:, cols], preferred_element_type=F32))
        for c in range(tm // CHUNK):
            rows = slice(c * CHUNK, (c + 1) * CHUNK)
            mix = jnp.dot(ws_ref[g], vn_ref[rows, cols], preferred_element_type=F32) + bs_ref[g]
            uv_ref[rows, cols] = (u[rows, :] * mix).astype(BF16)
    out = jnp.dot(uv_ref[...], wout_ref[...], preferred_element_type=F32)
    out_ref[...] = x + mod_ref[2:3, :] * out


def _gmlp(h, mod, ctx_row, w_in, norm_g, w_s, b_s, w_out, tm):
    b, l, d = h.shape
    gw = w_out.shape[0]
    kern = functools.partial(_gmlp_kernel, tm=tm, gw=gw)
    return pl.pallas_call(
        kern,
        grid=(b, l // tm),
        in_specs=[pl.BlockSpec((None, tm, d), lambda b, i: (b, i, 0)),
                  _mod_spec(d, ctx_row),
                  _resident(w_in.shape), _resident(norm_g.shape), _resident(w_s.shape),
                  _resident(b_s.shape), _resident(w_out.shape)],
        out_specs=pl.BlockSpec((None, tm, d), lambda b, i: (b, i, 0)),
        out_shape=jax.ShapeDtypeStruct(h.shape, F32),
        scratch_shapes=[pltpu.VMEM((tm, gw), BF16), pltpu.VMEM((tm, gw), BF16)],
        compiler_params=_params(2),
        name="gmlp",
    )(h, mod, w_in, norm_g, w_s, b_s, w_out)


def _ffn_kernel(*refs, tm, n_chunks, final):
    if final:
        h_ref, hp_ref, hn_ref, mod_ref, wup_ref, cw_ref, cb_ref, wdn_ref, fg_ref, out_ref, xm_ref, acc_ref = refs
    else:
        h_ref, hp_ref, hn_ref, mod_ref, wup_ref, cw_ref, cb_ref, wdn_ref, out_ref, xm_ref, acc_ref = refs
    i = pl.program_id(1)
    last = pl.num_programs(1) - 1
    x = h_ref[...]
    xm_ref[0:HALO, :] = jnp.where(i == 0, 0.0, _modulate(hp_ref[...], mod_ref, 3)).astype(BF16)
    xm_ref[HALO:HALO + tm, :] = _modulate(x, mod_ref, 3).astype(BF16)
    xm_ref[HALO + tm:, :] = jnp.where(i == last, 0.0, _modulate(hn_ref[...], mod_ref, 3)).astype(BF16)
    acc_ref[...] = jnp.zeros_like(acc_ref)
    rows = tm + 2 * HALO

    def chunk(c, carry):
        xm = xm_ref[...]

        def conv_half(s):
            z = jnp.dot(xm, wup_ref[s, c], preferred_element_type=F32)
            cw = cw_ref[s, c]
            zc = (pltpu.roll(z, 1, 0) * cw[0:1, :] + z * cw[1:2, :]
                  + pltpu.roll(z, rows - 1, 0) * cw[2:3, :] + cb_ref[s, c])
            return zc[HALO:HALO + tm, :]

        act = (_silu(conv_half(0)) * conv_half(1)).astype(BF16)
        acc_ref[...] += jnp.dot(act, wdn_ref[c], preferred_element_type=F32)
        return carry

    lax.fori_loop(0, n_chunks, chunk, 0)
    y = x + mod_ref[5:6, :] * acc_ref[...]
    if final:
        y = _rms(y) * fg_ref[...]
    out_ref[...] = y


def _ffn(h, mod, ctx_row, w_up, conv_w, conv_b, w_down, final_g, tm):
    b, l, d = h.shape
    n_chunks = w_down.shape[0]
    nh = l // HALO
    per = tm // HALO
    final = final_g is not None
    kern = functools.partial(_ffn_kernel, tm=tm, n_chunks=n_chunks, final=final)
    in_specs = [pl.BlockSpec((None, tm, d), lambda b, i: (b, i, 0)),
                pl.BlockSpec((None, HALO, d), lambda b, i: (b, jnp.maximum(i * per - 1, 0), 0)),
                pl.BlockSpec((None, HALO, d), lambda b, i: (b, jnp.minimum((i + 1) * per, nh - 1), 0)),
                _mod_spec(d, ctx_row),
                _resident(w_up.shape), _resident(conv_w.shape), _resident(conv_b.shape), _resident(w_down.shape)]
    args = [h, h, h, mod, w_up, conv_w, conv_b, w_down]
    if final:
        in_specs.append(_resident(final_g.shape))
        args.append(final_g)
    return pl.pallas_call(
        kern,
        grid=(b, l // tm),
        in_specs=in_specs,
        out_specs=pl.BlockSpec((None, tm, d), lambda b, i: (b, i, 0)),
        out_shape=jax.ShapeDtypeStruct(h.shape, F32),
        scratch_shapes=[pltpu.VMEM((tm + 2 * HALO, d), BF16), pltpu.VMEM((tm, d), F32)],
        compiler_params=_params(2),
        name="ffn",
    )(*args)


def _qkv_kernel(*refs, n_q, n_rope, n_blocks, group):
    if n_rope:
        h_ref, mod_ref, w_ref, cos_ref, sa_ref, sb_ref, out_ref = refs
    else:
        h_ref, mod_ref, w_ref, out_ref = refs
    xl = _modulate(h_ref[...], mod_ref, 0).astype(BF16)
    for j0 in range(0, n_blocks, group):
        y = jnp.dot(xl, w_ref[:, j0 * LANES:(j0 + group) * LANES], preferred_element_type=F32)
        for jj in range(group):
            j = j0 + jj
            yb = y[:, jj * LANES:(jj + 1) * LANES]
            if j < n_rope:
                yb = (yb * cos_ref[...] + pltpu.roll(yb, LANES - SUB_HEAD // 4, 1) * sa_ref[...]
                      + pltpu.roll(yb, SUB_HEAD // 4, 1) * sb_ref[...])
            if j < n_q:
                yb = yb * (SUB_HEAD ** -0.5)
            out_ref[:, j * LANES:(j + 1) * LANES] = yb.astype(BF16)


def _qkv(h, mod, ctx_row, w, tables, n_q, n_rope, tm):
    b, l, d = h.shape
    n = w.shape[1]
    kern = functools.partial(_qkv_kernel, n_q=n_q, n_rope=n_rope, n_blocks=n // LANES, group=4)
    in_specs = [pl.BlockSpec((None, tm, d), lambda b, i: (b, i, 0)), _mod_spec(d, ctx_row), _resident(w.shape)]
    args = [h, mod, w]
    if n_rope:
        in_specs += [pl.BlockSpec((tm, LANES), lambda b, i: (i, 0))] * 3
        args += list(tables)
    return pl.pallas_call(
        kern,
        grid=(b, l // tm),
        in_specs=in_specs,
        out_specs=pl.BlockSpec((None, tm, n), lambda b, i: (b, i, 0)),
        out_shape=jax.ShapeDtypeStruct((b, l, n), BF16),
        compiler_params=_params(2),
        name="qkv",
    )(*args)


def _rope_tables(n_lat):
    rows = n_lat // GRID_W
    row_pos = jnp.broadcast_to(jnp.arange(rows, dtype=F32)[:, None], (rows, GRID_W)).reshape(-1)
    col_pos = jnp.broadcast_to(jnp.arange(GRID_W, dtype=F32)[None, :], (rows, GRID_W)).reshape(-1)
    n_freq = SUB_HEAD // 4
    inv_freq = ROPE_THETA ** (-jnp.arange(n_freq, dtype=F32) / n_freq)
    ang_r = row_pos[:, None] * inv_freq
    ang_c = col_pos[:, None] * inv_freq
    ang = jnp.concatenate([ang_r, ang_r, ang_c, ang_c] * 2, axis=-1)
    first = (jnp.arange(LANES) % (2 * n_freq)) < n_freq
    sin = jnp.sin(ang)
    return jnp.cos(ang), jnp.where(first, -sin, 0.0), jnp.where(first, 0.0, sin)


def _attn_kernel(q_ref, kc_ref, vc_ref, kl_ref, vl_ref, lam_ref, g_ref, o_ref, qs_ref, m_ref, l_ref, acc_ref,
                 *, tq, tk, n_lat, lambda_init):
    lane = lax.broadcasted_iota(jnp.int32, (tq, HEAD_W), 1)
    q = q_ref[...]
    zero = jnp.zeros_like(q)
    qs_ref[0:tq, :] = jnp.where(lane < SUB_HEAD, q, zero)
    qs_ref[tq:2 * tq, :] = jnp.where(lane >= SUB_HEAD, q, zero)
    m_ref[...] = jnp.full_like(m_ref, -jnp.inf)
    l_ref[...] = jnp.zeros_like(l_ref)
    acc_ref[...] = jnp.zeros_like(acc_ref)

    def step(k, v):
        s = lax.dot_general(qs_ref[...], k, (((1,), (1,)), ((), ())), preferred_element_type=F32)
        m_prev = m_ref[...]
        m_new = jnp.maximum(m_prev, jnp.max(s, axis=1, keepdims=True))
        alpha = jnp.exp(m_prev - m_new)
        p = jnp.exp(s - m_new)
        l_ref[...] = alpha * l_ref[...] + jnp.sum(p, axis=1, keepdims=True)
        acc_ref[...] = alpha * acc_ref[...] + jnp.dot(p.astype(BF16), v, preferred_element_type=F32)
        m_ref[...] = m_new

    step(kc_ref[...], vc_ref[...])

    def body(j, carry):
        off = pl.multiple_of(j * tk, tk)
        step(kl_ref[pl.ds(off, tk), :], vl_ref[pl.ds(off, tk), :])
        return carry

    lax.fori_loop(0, n_lat // tk, body, 0)

    lp = lam_ref[...]
    lam = (jnp.exp(jnp.sum(lp[0:1, :] * lp[1:2, :], axis=1, keepdims=True))
           - jnp.exp(jnp.sum(lp[2:3, :] * lp[3:4, :], axis=1, keepdims=True)) + lambda_init)
    o = acc_ref[...] / l_ref[...]
    od = o[0:tq, :] - lam * o[tq:2 * tq, :]
    o_ref[...] = (_rms(od) * (g_ref[...] * (1.0 - lambda_init))).astype(BF16)


def _attn(qkv, kv_ctx, lam_params, subln_g, lambda_init, tq, tk):
    b, n_lat, n3 = qkv.shape
    n_ctx = kv_ctx.shape[1]
    heads = n3 // (3 * HEAD_W)
    kern = functools.partial(_attn_kernel, tq=tq, tk=tk, n_lat=n_lat, lambda_init=lambda_init)
    return pl.pallas_call(
        kern,
        grid=(b, heads, n_lat // tq),
        in_specs=[pl.BlockSpec((None, tq, HEAD_W), lambda b, h, i: (b, i, h)),
                  pl.BlockSpec((None, n_ctx, HEAD_W), lambda b, h, i: (b, 0, h)),
                  pl.BlockSpec((None, n_ctx, HEAD_W), lambda b, h, i: (b, 0, heads + h)),
                  pl.BlockSpec((None, n_lat, HEAD_W), lambda b, h, i: (b, 0, heads + h)),
                  pl.BlockSpec((None, n_lat, HEAD_W), lambda b, h, i: (b, 0, 2 * heads + h)),
                  pl.BlockSpec(lam_params.shape, lambda b, h, i: (0, 0)),
                  pl.BlockSpec(subln_g.shape, lambda b, h, i: (0, 0))],
        out_specs=pl.BlockSpec((None, tq, HEAD_W), lambda b, h, i: (b, i, h)),
        out_shape=jax.ShapeDtypeStruct((b, n_lat, heads * HEAD_W), BF16),
        scratch_shapes=[pltpu.VMEM((2 * tq, HEAD_W), BF16), pltpu.VMEM((2 * tq, 1), F32),
                        pltpu.VMEM((2 * tq, 1), F32), pltpu.VMEM((2 * tq, HEAD_W), F32)],
        compiler_params=_params(3),
        name="attn",
    )(qkv, kv_ctx, kv_ctx, qkv, qkv, lam_params, subln_g)


def _proj_kernel(o_ref, h_ref, mod_ref, w_ref, out_ref):
    out_ref[...] = h_ref[...] + mod_ref[2:3, :] * jnp.dot(o_ref[...], w_ref[...], preferred_element_type=F32)


def _proj(o, h, mod, w, tm):
    b, l, d = h.shape
    return pl.pallas_call(
        _proj_kernel,
        grid=(b, l // tm),
        in_specs=[pl.BlockSpec((None, tm, o.shape[2]), lambda b, i: (b, i, 0)),
                  pl.BlockSpec((None, tm, d), lambda b, i: (b, i, 0)),
                  _mod_spec(d, None), _resident(w.shape)],
        out_specs=pl.BlockSpec((None, tm, d), lambda b, i: (b, i, 0)),
        out_shape=jax.ShapeDtypeStruct(h.shape, F32),
        compiler_params=_params(2),
        name="proj",
    )(o, h, mod, w)


def _tile(l, target):
    return min(l, target)


def kernel(x, c, ctx, c_ctx, ada_w, ada_b, gm_w_in, gm_norm_g, gm_w_s, gm_b_s, gm_w_out, da_w_qkv, da_lambda_q1, da_lambda_k1, da_lambda_q2, da_lambda_k2, da_subln_g, da_w_out, ffn_w_up, ffn_conv_w, ffn_conv_b, ffn_w_down, final_norm_g):
    bsz, n_lat, d = x.shape
    depth = ada_w.shape[0]
    ffn_dim = ffn_w_down.shape[1]
    assert depth == 2 and bsz < ADA_ROWS and n_lat % GRID_W == 0
    ctx_row = bsz
    ffn_cw = 2 * LANES
    n_chunks = ffn_dim // ffn_cw
    assert n_chunks * ffn_cw == ffn_dim

    cond = jnp.zeros((ADA_ROWS, d), F32).at[:bsz].set(c).at[ctx_row].set(c_ctx)
    mod = _ada(cond, ada_w, ada_b).reshape(depth, ADA_ROWS, 6, d)

    def ffn_weights(i):
        w_up = ffn_w_up[i].astype(BF16).reshape(d, 2, n_chunks, ffn_cw).transpose(1, 2, 0, 3)
        cw = ffn_conv_w[i].reshape(CONV_W, 2, n_chunks, ffn_cw).transpose(1, 2, 0, 3)
        cb = ffn_conv_b[i].reshape(2, n_chunks, 1, ffn_cw)
        w_dn = ffn_w_down[i].astype(BF16).reshape(n_chunks, ffn_cw, d)
        return w_up, cw, cb, w_dn

    tm_lat = _tile(n_lat, 512)
    tm_ctx = _tile(ctx.shape[1], 512)

    gm = (gm_w_in[0].astype(BF16), gm_norm_g[0].reshape(1, -1), gm_w_s[0].astype(BF16),
          gm_b_s[0][:, :, None], gm_w_out[0].astype(BF16))
    fw = ffn_weights(0)
    h = _gmlp(x, mod[0], None, *gm, tm_lat)
    hc = _gmlp(ctx, mod[0], ctx_row, *gm, tm_ctx)
    h = _ffn(h, mod[0], None, *fw, None, tm_lat)
    hc = _ffn(hc, mod[0], ctx_row, *fw, None, tm_ctx)

    lambda_init = 0.8 - 0.6 * math.exp(-0.3 * 1)
    w_qkv = da_w_qkv[0].astype(BF16)
    da_width = w_qkv.shape[1] // 3
    n_head_blocks = da_width // LANES
    qkv = _qkv(h, mod[1], None, w_qkv, _rope_tables(n_lat), n_head_blocks, 2 * n_head_blocks, tm_lat)
    kv_ctx = _qkv(hc, mod[1], ctx_row, w_qkv[:, da_width:], None, 0, 0, tm_ctx)
    lam_params = jnp.stack([da_lambda_q1[0], da_lambda_k1[0], da_lambda_q2[0], da_lambda_k2[0]])
    o = _attn(qkv, kv_ctx, lam_params, da_subln_g[0].reshape(1, -1), lambda_init, 256, 512)
    h = _proj(o, h, mod[1], da_w_out[0].astype(BF16), tm_lat)
    fw = ffn_weights(1)
    return _ffn(h, mod[1], None, *fw, final_norm_g.reshape(1, -1), tm_lat)
```

```python
import functools
import math

import jax
import jax.numpy as jnp
from jax import lax
from jax.experimental import pallas as pl
from jax.experimental.pallas import tpu as pltpu

EPS = 1e-6
GRID_W = 64
GM_GROUPS = 8
CHUNK = 128
SUB_HEAD = 64
HEAD_W = 2 * SUB_HEAD
VT_ROWS = HEAD_W + 16
ROPE_THETA = 10000.0
CONV_W = 3
HALO = 16
LANES = 128
ADA_ROWS = 16
VMEM_LIMIT_BYTES = 56 * 1024 * 1024

F32 = jnp.float32
BF16 = jnp.bfloat16


def _rms(x):
    return x * lax.rsqrt(jnp.mean(x * x, axis=-1, keepdims=True) + EPS)


def _modulate(x, mod_ref, k):
    return _rms(x) * (1.0 + mod_ref[k + 1:k + 2, :]) + mod_ref[k:k + 1, :]


def _silu(x):
    return x * (1.0 / (1.0 + jnp.exp(-x)))


def _gelu_tanh(x):
    return 0.5 * x * (1.0 + jnp.tanh(math.sqrt(2.0 / math.pi) * (x + 0.044715 * (x * x * x))))


def _params(n_axes):
    return pltpu.CompilerParams(dimension_semantics=("arbitrary",) * n_axes,
                                vmem_limit_bytes=VMEM_LIMIT_BYTES)


def _resident(shape):
    nd = len(shape)
    return pl.BlockSpec(shape, lambda *_: (0,) * nd, pipeline_mode=pl.Buffered(1))


def _ada_kernel(cond_ref, w_ref, b_ref, out_ref):
    s = _silu(cond_ref[...])
    out_ref[...] = jnp.dot(s, w_ref[...], preferred_element_type=F32,
                           precision=lax.Precision.HIGHEST) + b_ref[...]


def _ada(cond, ada_w, ada_b):
    depth, d, n = ada_w.shape
    nb = 6 * LANES * 2
    return pl.pallas_call(
        _ada_kernel,
        grid=(depth, n // nb),
        in_specs=[pl.BlockSpec((ADA_ROWS, d), lambda l, j: (0, 0)),
                  pl.BlockSpec((None, d, nb), lambda l, j: (l, 0, j)),
                  pl.BlockSpec((None, 1, nb), lambda l, j: (l, 0, j))],
        out_specs=pl.BlockSpec((None, ADA_ROWS, nb), lambda l, j: (l, 0, j)),
        out_shape=jax.ShapeDtypeStruct((depth, ADA_ROWS, n), F32),
        compiler_params=_params(2),
        name="ada",
    )(cond, ada_w, ada_b.reshape(depth, 1, n))


def _mod_spec(d, ctx_row):
    if ctx_row is None:
        return pl.BlockSpec((None, 6, d), lambda b, i: (b, 0, 0))
    return pl.BlockSpec((None, 6, d), lambda b, i: (ctx_row, 0, 0))


def _gmlp_kernel(h_ref, mod_ref, win_ref, ng_ref, ws_ref, bs_ref, wout_ref, out_ref, vn_ref, uv_ref, *, tm, gw):
    x = h_ref[...]
    xl = _modulate(x, mod_ref, 0).astype(BF16)
    v = _gelu_tanh(jnp.dot(xl, win_ref[:, gw:], preferred_element_type=F32))
    vn_ref[...] = (_rms(v) * ng_ref[...]).astype(BF16)
    gd = gw // GM_GROUPS
    for g in range(GM_GROUPS):
        cols = slice(g * gd, (g + 1) * gd)
        u = _gelu_tanh(jnp.dot(xl, win_ref[:, cols], preferred_element_type=F32))
        for c in range(tm // CHUNK):
            rows = slice(c * CHUNK, (c + 1) * CHUNK)
            mix = jnp.dot(ws_ref[g], vn_ref[rows, cols], preferred_element_type=F32) + bs_ref[g]
            uv_ref[rows, cols] = (u[rows, :] * mix).astype(BF16)
    out = jnp.dot(uv_ref[...], wout_ref[...], preferred_element_type=F32)
    out_ref[...] = x + mod_ref[2:3, :] * out


def _gmlp(h, mod, ctx_row, w_in, norm_g, w_s, b_s, w_out, tm):
    b, l, d = h.shape
    gw = w_out.shape[0]
    kern = functools.partial(_gmlp_kernel, tm=tm, gw=gw)
    return pl.pallas_call(
        kern,
        grid=(b, l // tm),
        in_specs=[pl.BlockSpec((None, tm, d), lambda b, i: (b, i, 0)),
                  _mod_spec(d, ctx_row),
                  _resident(w_in.shape), _resident(norm_g.shape), _resident(w_s.shape),
                  _resident(b_s.shape), _resident(w_out.shape)],
        out_specs=pl.BlockSpec((None, tm, d), lambda b, i: (b, i, 0)),
        out_shape=jax.ShapeDtypeStruct(h.shape, F32),
        scratch_shapes=[pltpu.VMEM((tm, gw), BF16), pltpu.VMEM((tm, gw), BF16)],
        compiler_params=_params(2),
        name="gmlp",
    )(h, mod, w_in, norm_g, w_s, b_s, w_out)


def _ffn_kernel(*refs, tm, n_chunks, final):
    if final:
        h_ref, hp_ref, hn_ref, mod_ref, wup_ref, cw_ref, cb_ref, wdn_ref, fg_ref, out_ref, xm_ref, acc_ref = refs
    else:
        h_ref, hp_ref, hn_ref, mod_ref, wup_ref, cw_ref, cb_ref, wdn_ref, out_ref, xm_ref, acc_ref = refs
    i = pl.program_id(1)
    last = pl.num_programs(1) - 1
    x = h_ref[...]
    xm_ref[0:HALO, :] = jnp.where(i == 0, 0.0, _modulate(hp_ref[...], mod_ref, 3)).astype(BF16)
    xm_ref[HALO:HALO + tm, :] = _modulate(x, mod_ref, 3).astype(BF16)
    xm_ref[HALO + tm:, :] = jnp.where(i == last, 0.0, _modulate(hn_ref[...], mod_ref, 3)).astype(BF16)
    acc_ref[...] = jnp.zeros_like(acc_ref)
    rows = tm + 2 * HALO

    def chunk(c, carry):
        xm = xm_ref[...]

        def conv_half(s):
            z = jnp.dot(xm, wup_ref[s, c], preferred_element_type=F32)
            cw = cw_ref[s, c]
            zc = (pltpu.roll(z, 1, 0) * cw[0:1, :] + z * cw[1:2, :]
                  + pltpu.roll(z, rows - 1, 0) * cw[2:3, :] + cb_ref[s, c])
            return zc[HALO:HALO + tm, :]

        act = (_silu(conv_half(0)) * conv_half(1)).astype(BF16)
        acc_ref[...] += jnp.dot(act, wdn_ref[c], preferred_element_type=F32)
        return carry

    lax.fori_loop(0, n_chunks, chunk, 0)
    y = x + mod_ref[5:6, :] * acc_ref[...]
    if final:
        y = _rms(y) * fg_ref[...]
    out_ref[...] = y


def _ffn(h, mod, ctx_row, w_up, conv_w, conv_b, w_down, final_g, tm):
    b, l, d = h.shape
    n_chunks = w_down.shape[0]
    nh = l // HALO
    per = tm // HALO
    final = final_g is not None
    kern = functools.partial(_ffn_kernel, tm=tm, n_chunks=n_chunks, final=final)
    in_specs = [pl.BlockSpec((None, tm, d), lambda b, i: (b, i, 0)),
                pl.BlockSpec((None, HALO, d), lambda b, i: (b, jnp.maximum(i * per - 1, 0), 0)),
                pl.BlockSpec((None, HALO, d), lambda b, i: (b, jnp.minimum((i + 1) * per, nh - 1), 0)),
                _mod_spec(d, ctx_row),
                _resident(w_up.shape), _resident(conv_w.shape), _resident(conv_b.shape), _resident(w_down.shape)]
    args = [h, h, h, mod, w_up, conv_w, conv_b, w_down]
    if final:
        in_specs.append(_resident(final_g.shape))
        args.append(final_g)
    return pl.pallas_call(
        kern,
        grid=(b, l // tm),
        in_specs=in_specs,
        out_specs=pl.BlockSpec((None, tm, d), lambda b, i: (b, i, 0)),
        out_shape=jax.ShapeDtypeStruct(h.shape, F32),
        scratch_shapes=[pltpu.VMEM((tm + 2 * HALO, d), BF16), pltpu.VMEM((tm, d), F32)],
        compiler_params=_params(2),
        name="ffn",
    )(*args)


def _qkv_kernel(*refs, heads, with_q):
    if with_q:
        h_ref, mod_ref, w_ref, cos_ref, sa_ref, sb_ref, qt_ref, k_ref, vt_ref = refs
    else:
        h_ref, mod_ref, w_ref, k_ref, vt_ref = refs
    xl = _modulate(h_ref[...], mod_ref, 0).astype(BF16)

    def rope(yb):
        return (yb * cos_ref[...] + pltpu.roll(yb, LANES - SUB_HEAD // 4, 1) * sa_ref[...]
                + pltpu.roll(yb, SUB_HEAD // 4, 1) * sb_ref[...])

    group = 4
    for part in range(3 if with_q else 2):
        kind = part if with_q else part + 1
        for h0 in range(0, heads, group):
            c0 = (part * heads + h0) * HEAD_W
            y = jnp.dot(xl, w_ref[:, c0:c0 + group * HEAD_W], preferred_element_type=F32)
            for hh in range(group):
                h = h0 + hh
                yb = y[:, hh * HEAD_W:(hh + 1) * HEAD_W]
                if kind == 0:
                    qt_ref[h] = (rope(yb) * (SUB_HEAD ** -0.5 * math.log2(math.e))).T.astype(BF16)
                elif kind == 1:
                    k_ref[:, h * HEAD_W:(h + 1) * HEAD_W] = (rope(yb) if with_q else yb).astype(BF16)
                else:
                    vt_ref[h, 0:HEAD_W, :] = yb.T.astype(BF16)
                    vt_ref[h, HEAD_W:VT_ROWS, :] = jnp.ones((VT_ROWS - HEAD_W, yb.shape[0]), BF16)


def _qkv(h, mod, ctx_row, w, tables, tm):
    b, l, d = h.shape
    with_q = tables is not None
    heads = w.shape[1] // HEAD_W // (3 if with_q else 2)
    kern = functools.partial(_qkv_kernel, heads=heads, with_q=with_q)
    in_specs = [pl.BlockSpec((None, tm, d), lambda b, i: (b, i, 0)), _mod_spec(d, ctx_row), _resident(w.shape)]
    args = [h, mod, w]
    out_specs = [pl.BlockSpec((None, tm, heads * HEAD_W), lambda b, i: (b, i, 0)),
                 pl.BlockSpec((None, heads, None, VT_ROWS, tm), lambda b, i: (b, 0, i, 0, 0))]
    out_shape = [jax.ShapeDtypeStruct((b, l, heads * HEAD_W), BF16),
                 jax.ShapeDtypeStruct((b, heads, l // tm, VT_ROWS, tm), BF16)]
    if with_q:
        in_specs += [pl.BlockSpec((tm, LANES), lambda b, i: (i, 0))] * 3
        args += list(tables)
        out_specs.insert(0, pl.BlockSpec((None, heads, HEAD_W, tm), lambda b, i: (b, 0, 0, i)))
        out_shape.insert(0, jax.ShapeDtypeStruct((b, heads, HEAD_W, l), BF16))
    outs = pl.pallas_call(
        kern,
        grid=(b, l // tm),
        in_specs=in_specs,
        out_specs=out_specs,
        out_shape=out_shape,
        compiler_params=_params(2),
        name="qkv",
    )(*args)
    return outs if with_q else (None, *outs)


def _rope_tables(n_lat):
    rows = n_lat // GRID_W
    row_pos = jnp.broadcast_to(jnp.arange(rows, dtype=F32)[:, None], (rows, GRID_W)).reshape(-1)
    col_pos = jnp.broadcast_to(jnp.arange(GRID_W, dtype=F32)[None, :], (rows, GRID_W)).reshape(-1)
    n_freq = SUB_HEAD // 4
    inv_freq = ROPE_THETA ** (-jnp.arange(n_freq, dtype=F32) / n_freq)
    ang_r = row_pos[:, None] * inv_freq
    ang_c = col_pos[:, None] * inv_freq
    ang = jnp.concatenate([ang_r, ang_r, ang_c, ang_c] * 2, axis=-1)
    first = (jnp.arange(LANES) % (2 * n_freq)) < n_freq
    sin = jnp.sin(ang)
    return jnp.cos(ang), jnp.where(first, -sin, 0.0), jnp.where(first, 0.0, sin)


def _attn_kernel(qt_ref, kc_ref, vtc_ref, kl_ref, vtl_ref, lam_ref, g_ref, o_ref, qs_ref, s_ref, mx_ref, m_ref, acc_ref,
                 *, tq, lambda_init):
    dim = lax.broadcasted_iota(jnp.int32, (HEAD_W, tq), 0)
    qt = qt_ref[...]
    zero = jnp.zeros_like(qt)
    qs_ref[:, 0:tq] = jnp.where(dim < SUB_HEAD, qt, zero)
    qs_ref[:, tq:2 * tq] = jnp.where(dim >= SUB_HEAD, qt, zero)
    m_ref[...] = jnp.full_like(m_ref, -jnp.inf)
    acc_ref[...] = jnp.zeros_like(acc_ref)
    n_chunks, _, tk = vtl_ref.shape
    n_ctx = kc_ref.shape[0]

    def stage(slot, k):
        st = jnp.dot(k, qs_ref[...], preferred_element_type=F32)
        s_ref[slot, 0:k.shape[0], :] = st
        mx_ref[slot] = jnp.max(st, axis=0, keepdims=True)

    def consume(slot, n, vt):
        m_prev = m_ref[...]
        m_new = jnp.maximum(m_prev, mx_ref[slot])
        alpha = jnp.exp2(m_prev - m_new)
        pt = jnp.exp2(s_ref[slot, 0:n, :] - m_new).astype(BF16)
        acc_ref[...] = alpha * acc_ref[...] + jnp.dot(vt, pt, preferred_element_type=F32)
        m_ref[...] = m_new

    def lat_k(j):
        return kl_ref[pl.ds(pl.multiple_of(j * tk, tk), tk), :]

    stage(1, kc_ref[...])
    stage(0, lat_k(0))
    consume(1, n_ctx, vtc_ref[0])

    def pair(jj, carry):
        j = 2 * jj + 1
        stage(1, lat_k(j))
        consume(0, tk, vtl_ref[j - 1])
        stage(0, lat_k(j + 1))
        consume(1, tk, vtl_ref[j])
        return carry

    lax.fori_loop(0, (n_chunks - 2) // 2, pair, 0)
    stage(1, lat_k(n_chunks - 1))
    consume(0, tk, vtl_ref[n_chunks - 2])
    consume(1, tk, vtl_ref[n_chunks - 1])

    lp = lam_ref[...]
    lam = (jnp.exp(jnp.sum(lp[0:1, :] * lp[1:2, :], axis=1, keepdims=True))
           - jnp.exp(jnp.sum(lp[2:3, :] * lp[3:4, :], axis=1, keepdims=True)) + lambda_init)
    ot = acc_ref[0:HEAD_W, :] / acc_ref[HEAD_W:HEAD_W + 1, :]
    od = ot[:, 0:tq] - lam * ot[:, tq:2 * tq]
    od = od * lax.rsqrt(jnp.mean(od * od, axis=0, keepdims=True) + EPS) * (g_ref[...] * (1.0 - lambda_init))
    o_ref[...] = od.T.astype(BF16)


def _attn(qt, k_ctx, vt_ctx, k_lat, vt_lat, lam_params, subln_g, lambda_init, tq):
    b, heads, _, n_lat = qt.shape
    n_ctx = k_ctx.shape[1]
    n_chunks, _, tk = vt_lat.shape[2:]
    assert n_chunks % 2 == 0 and n_ctx <= tk
    kern = functools.partial(_attn_kernel, tq=tq, lambda_init=lambda_init)
    vt_block = lambda a: pl.BlockSpec((None, None) + a.shape[2:], lambda b, h, i: (b, h, 0, 0, 0))
    return pl.pallas_call(
        kern,
        grid=(b, heads, n_lat // tq),
        in_specs=[pl.BlockSpec((None, None, HEAD_W, tq), lambda b, h, i: (b, h, 0, i)),
                  pl.BlockSpec((None, n_ctx, HEAD_W), lambda b, h, i: (b, 0, h)),
                  vt_block(vt_ctx),
                  pl.BlockSpec((None, n_lat, HEAD_W), lambda b, h, i: (b, 0, h)),
                  vt_block(vt_lat),
                  pl.BlockSpec(lam_params.shape, lambda b, h, i: (0, 0)),
                  pl.BlockSpec(subln_g.shape, lambda b, h, i: (0, 0))],
        out_specs=pl.BlockSpec((None, tq, HEAD_W), lambda b, h, i: (b, i, h)),
        out_shape=jax.ShapeDtypeStruct((b, n_lat, heads * HEAD_W), BF16),
        scratch_shapes=[pltpu.VMEM((HEAD_W, 2 * tq), BF16), pltpu.VMEM((2, tk, 2 * tq), F32),
                        pltpu.VMEM((2, 1, 2 * tq), F32), pltpu.VMEM((1, 2 * tq), F32),
                        pltpu.VMEM((VT_ROWS, 2 * tq), F32)],
        compiler_params=_params(3),
        name="attn",
    )(qt, k_ctx, vt_ctx, k_lat, vt_lat, lam_params, subln_g)


def _proj_kernel(o_ref, h_ref, mod_ref, w_ref, out_ref):
    out_ref[...] = h_ref[...] + mod_ref[2:3, :] * jnp.dot(o_ref[...], w_ref[...], preferred_element_type=F32)


def _proj(o, h, mod, w, tm):
    b, l, d = h.shape
    return pl.pallas_call(
        _proj_kernel,
        grid=(b, l // tm),
        in_specs=[pl.BlockSpec((None, tm, o.shape[2]), lambda b, i: (b, i, 0)),
                  pl.BlockSpec((None, tm, d), lambda b, i: (b, i, 0)),
                  _mod_spec(d, None), _resident(w.shape)],
        out_specs=pl.BlockSpec((None, tm, d), lambda b, i: (b, i, 0)),
        out_shape=jax.ShapeDtypeStruct(h.shape, F32),
        compiler_params=_params(2),
        name="proj",
    )(o, h, mod, w)


def _tile(l, target):
    return min(l, target)


def kernel(x, c, ctx, c_ctx, ada_w, ada_b, gm_w_in, gm_norm_g, gm_w_s, gm_b_s, gm_w_out, da_w_qkv, da_lambda_q1, da_lambda_k1, da_lambda_q2, da_lambda_k2, da_subln_g, da_w_out, ffn_w_up, ffn_conv_w, ffn_conv_b, ffn_w_down, final_norm_g):
    bsz, n_lat, d = x.shape
    depth = ada_w.shape[0]
    ffn_dim = ffn_w_down.shape[1]
    assert depth == 2 and bsz < ADA_ROWS and n_lat % GRID_W == 0
    ctx_row = bsz
    ffn_cw = 2 * LANES
    n_chunks = ffn_dim // ffn_cw
    assert n_chunks * ffn_cw == ffn_dim

    cond = jnp.zeros((ADA_ROWS, d), F32).at[:bsz].set(c).at[ctx_row].set(c_ctx)
    mod = _ada(cond, ada_w, ada_b).reshape(depth, ADA_ROWS, 6, d)

    def ffn_weights(i):
        w_up = ffn_w_up[i].astype(BF16).reshape(d, 2, n_chunks, ffn_cw).transpose(1, 2, 0, 3)
        cw = ffn_conv_w[i].reshape(CONV_W, 2, n_chunks, ffn_cw).transpose(1, 2, 0, 3)
        cb = ffn_conv_b[i].reshape(2, n_chunks, 1, ffn_cw)
        w_dn = ffn_w_down[i].astype(BF16).reshape(n_chunks, ffn_cw, d)
        return w_up, cw, cb, w_dn

    tm_lat = _tile(n_lat, 512)
    tm_ctx = _tile(ctx.shape[1], 512)

    gm = (gm_w_in[0].astype(BF16), gm_norm_g[0].reshape(1, -1), gm_w_s[0].astype(BF16),
          gm_b_s[0][:, :, None], gm_w_out[0].astype(BF16))
    fw = ffn_weights(0)
    h = _gmlp(x, mod[0], None, *gm, tm_lat)
    hc = _gmlp(ctx, mod[0], ctx_row, *gm, tm_ctx)
    h = _ffn(h, mod[0], None, *fw, None, tm_lat)
    hc = _ffn(hc, mod[0], ctx_row, *fw, None, tm_ctx)

    lambda_init = 0.8 - 0.6 * math.exp(-0.3 * 1)
    w_qkv = da_w_qkv[0].astype(BF16)
    da_width = w_qkv.shape[1] // 3
    qt, k_lat, vt_lat = _qkv(h, mod[1], None, w_qkv, _rope_tables(n_lat), tm_lat)
    _, k_ctx, vt_ctx = _qkv(hc, mod[1], ctx_row, w_qkv[:, da_width:], None, tm_ctx)
    lam_params = jnp.stack([da_lambda_q1[0], da_lambda_k1[0], da_lambda_q2[0], da_lambda_k2[0]])
    o = _attn(qt, k_ctx, vt_ctx, k_lat, vt_lat, lam_params, da_subln_g[0].reshape(-1, 1), lambda_init, 512)
    h = _proj(o, h, mod[1], da_w_out[0].astype(BF16), tm_lat)
    fw = ffn_weights(1)
    return _ffn(h, mod[1], None, *fw, final_norm_g.reshape(1, -1), tm_lat)
```

```python
import functools
import math

import jax
import jax.numpy as jnp
from jax import lax
from jax.experimental import pallas as pl
from jax.experimental.pallas import tpu as pltpu

EPS = 1e-6
GRID_W = 64
GM_GROUPS = 8
CHUNK = 128
SUB_HEAD = 64
HEAD_W = 2 * SUB_HEAD
VT_ROWS = HEAD_W + 16
ROPE_THETA = 10000.0
CONV_W = 3
HALO = 16
LANES = 128
ADA_ROWS = 16
VMEM_LIMIT_BYTES = 56 * 1024 * 1024

F32 = jnp.float32
BF16 = jnp.bfloat16


def _rms(x):
    return x * lax.rsqrt(jnp.mean(x * x, axis=-1, keepdims=True) + EPS)


def _modulate(x, mod_ref, k):
    return _rms(x) * (1.0 + mod_ref[k + 1:k + 2, :]) + mod_ref[k:k + 1, :]


def _silu(x):
    return x * (1.0 / (1.0 + jnp.exp(-x)))


def _gelu_tanh(x):
    return 0.5 * x * (1.0 + jnp.tanh(math.sqrt(2.0 / math.pi) * (x + 0.044715 * (x * x * x))))


def _params(n_axes):
    return pltpu.CompilerParams(dimension_semantics=("arbitrary",) * n_axes,
                                vmem_limit_bytes=VMEM_LIMIT_BYTES)


def _resident(shape):
    nd = len(shape)
    return pl.BlockSpec(shape, lambda *_: (0,) * nd, pipeline_mode=pl.Buffered(1))


def _ada_kernel(cond_ref, w_ref, b_ref, out_ref):
    s = _silu(cond_ref[...])
    out_ref[...] = jnp.dot(s, w_ref[...], preferred_element_type=F32,
                           precision=lax.Precision.HIGHEST) + b_ref[...]


def _ada(cond, ada_w, ada_b):
    depth, d, n = ada_w.shape
    nb = 6 * LANES * 2
    return pl.pallas_call(
        _ada_kernel,
        grid=(depth, n // nb),
        in_specs=[pl.BlockSpec((ADA_ROWS, d), lambda l, j: (0, 0)),
                  pl.BlockSpec((None, d, nb), lambda l, j: (l, 0, j)),
                  pl.BlockSpec((None, 1, nb), lambda l, j: (l, 0, j))],
        out_specs=pl.BlockSpec((None, ADA_ROWS, nb), lambda l, j: (l, 0, j)),
        out_shape=jax.ShapeDtypeStruct((depth, ADA_ROWS, n), F32),
        compiler_params=_params(2),
        name="ada",
    )(cond, ada_w, ada_b.reshape(depth, 1, n))


def _mod_spec(d, ctx_row):
    if ctx_row is None:
        return pl.BlockSpec((None, 6, d), lambda b, i: (b, 0, 0))
    return pl.BlockSpec((None, 6, d), lambda b, i: (ctx_row, 0, 0))


def _gmlp_kernel(h_ref, mod_ref, win_ref, ng_ref, ws_ref, bs_ref, wout_ref, out_ref, vn_ref, uv_ref, *, tm, gw):
    x = h_ref[...]
    xl = _modulate(x, mod_ref, 0).astype(BF16)
    v = _gelu_tanh(jnp.dot(xl, win_ref[:, gw:], preferred_element_type=F32))
    vn_ref[...] = (_rms(v) * ng_ref[...]).astype(BF16)
    gd = gw // GM_GROUPS
    for g in range(GM_GROUPS):
        cols = slice(g * gd, (g + 1) * gd)
        u = _gelu_tanh(jnp.dot(xl, win_ref[:, cols], preferred_element_type=F32))
        for c in range(tm // CHUNK):
            rows = slice(c * CHUNK, (c + 1) * CHUNK)
            mix = jnp.dot(ws_ref[g], vn_ref[rows, cols], preferred_element_type=F32) + bs_ref[g]
            uv_ref[rows, cols] = (u[rows, :] * mix).astype(BF16)
    out = jnp.dot(uv_ref[...], wout_ref[...], preferred_element_type=F32)
    out_ref[...] = x + mod_ref[2:3, :] * out


def _gmlp(h, mod, ctx_row, w_in, norm_g, w_s, b_s, w_out, tm):
    b, l, d = h.shape
    gw = w_out.shape[0]
    kern = functools.partial(_gmlp_kernel, tm=tm, gw=gw)
    return pl.pallas_call(
        kern,
        grid=(b, l // tm),
        in_specs=[pl.BlockSpec((None, tm, d), lambda b, i: (b, i, 0)),
                  _mod_spec(d, ctx_row),
                  _resident(w_in.shape), _resident(norm_g.shape), _resident(w_s.shape),
                  _resident(b_s.shape), _resident(w_out.shape)],
        out_specs=pl.BlockSpec((None, tm, d), lambda b, i: (b, i, 0)),
        out_shape=jax.ShapeDtypeStruct(h.shape, F32),
        scratch_shapes=[pltpu.VMEM((tm, gw), BF16), pltpu.VMEM((tm, gw), BF16)],
        compiler_params=_params(2),
        name="gmlp",
    )(h, mod, w_in, norm_g, w_s, b_s, w_out)


def _ffn_kernel(*refs, tm, n_chunks, final):
    if final:
        h_ref, hp_ref, hn_ref, mod_ref, wup_ref, cw_ref, cb_ref, wdn_ref, fg_ref, out_ref, xm_ref, acc_ref, z_ref = refs
    else:
        h_ref, hp_ref, hn_ref, mod_ref, wup_ref, cw_ref, cb_ref, wdn_ref, out_ref, xm_ref, acc_ref, z_ref = refs
    i = pl.program_id(1)
    last = pl.num_programs(1) - 1
    x = h_ref[...]
    xm_ref[0:HALO, :] = jnp.where(i == 0, 0.0, _modulate(hp_ref[...], mod_ref, 3)).astype(BF16)
    xm_ref[HALO:HALO + tm, :] = _modulate(x, mod_ref, 3).astype(BF16)
    xm_ref[HALO + tm:, :] = jnp.where(i == last, 0.0, _modulate(hn_ref[...], mod_ref, 3)).astype(BF16)
    acc_ref[...] = jnp.zeros_like(acc_ref)

    def stage(slot, c):
        xm = xm_ref[...]
        for s in range(2):
            z_ref[slot, s] = jnp.dot(xm, wup_ref[s, c], preferred_element_type=F32)

    def consume(slot, c):
        def conv_half(s):
            cw = cw_ref[s, c]
            taps = [z_ref[slot, s, HALO - 1 + j:HALO - 1 + j + tm, :] * cw[j:j + 1, :] for j in range(CONV_W)]
            return taps[0] + taps[1] + taps[2] + cb_ref[s, c]

        act = (_silu(conv_half(0)) * conv_half(1)).astype(BF16)
        acc_ref[...] += jnp.dot(act, wdn_ref[c], preferred_element_type=F32)

    def pair(jj, carry):
        c = 2 * jj
        stage(1, c + 1)
        consume(0, c)
        stage(0, c + 2)
        consume(1, c + 1)
        return carry

    n_pairs = (n_chunks - 1) // 2
    stage(0, 0)
    lax.fori_loop(0, n_pairs, pair, 0)
    if n_chunks % 2 == 0:
        stage(1, n_chunks - 1)
        consume(0, n_chunks - 2)
        consume(1, n_chunks - 1)
    else:
        consume(0, n_chunks - 1)
    y = x + mod_ref[5:6, :] * acc_ref[...]
    if final:
        y = _rms(y) * fg_ref[...]
    out_ref[...] = y


def _ffn(h, mod, ctx_row, w_up, conv_w, conv_b, w_down, final_g, tm):
    b, l, d = h.shape
    n_chunks = w_down.shape[0]
    nh = l // HALO
    per = tm // HALO
    final = final_g is not None
    kern = functools.partial(_ffn_kernel, tm=tm, n_chunks=n_chunks, final=final)
    in_specs = [pl.BlockSpec((None, tm, d), lambda b, i: (b, i, 0)),
                pl.BlockSpec((None, HALO, d), lambda b, i: (b, jnp.maximum(i * per - 1, 0), 0)),
                pl.BlockSpec((None, HALO, d), lambda b, i: (b, jnp.minimum((i + 1) * per, nh - 1), 0)),
                _mod_spec(d, ctx_row),
                _resident(w_up.shape), _resident(conv_w.shape), _resident(conv_b.shape), _resident(w_down.shape)]
    args = [h, h, h, mod, w_up, conv_w, conv_b, w_down]
    if final:
        in_specs.append(_resident(final_g.shape))
        args.append(final_g)
    return pl.pallas_call(
        kern,
        grid=(b, l // tm),
        in_specs=in_specs,
        out_specs=pl.BlockSpec((None, tm, d), lambda b, i: (b, i, 0)),
        out_shape=jax.ShapeDtypeStruct(h.shape, F32),
        scratch_shapes=[pltpu.VMEM((tm + 2 * HALO, d), BF16), pltpu.VMEM((tm, d), F32),
                        pltpu.VMEM((2, 2, tm + 2 * HALO, w_up.shape[3]), F32)],
        compiler_params=_params(2),
        name="ffn",
    )(*args)


def _qkv_kernel(*refs, heads, with_q):
    if with_q:
        h_ref, mod_ref, w_ref, cos_ref, sa_ref, sb_ref, qt_ref, k_ref, vt_ref = refs
    else:
        h_ref, mod_ref, w_ref, k_ref, vt_ref = refs
    xl = _modulate(h_ref[...], mod_ref, 0).astype(BF16)

    def rope(yb):
        return (yb * cos_ref[...] + pltpu.roll(yb, LANES - SUB_HEAD // 4, 1) * sa_ref[...]
                + pltpu.roll(yb, SUB_HEAD // 4, 1) * sb_ref[...])

    group = 4
    for part in range(3 if with_q else 2):
        kind = part if with_q else part + 1
        for h0 in range(0, heads, group):
            c0 = (part * heads + h0) * HEAD_W
            y = jnp.dot(xl, w_ref[:, c0:c0 + group * HEAD_W], preferred_element_type=F32)
            for hh in range(group):
                h = h0 + hh
                yb = y[:, hh * HEAD_W:(hh + 1) * HEAD_W]
                if kind == 0:
                    qt_ref[h] = (rope(yb) * (SUB_HEAD ** -0.5 * math.log2(math.e))).T.astype(BF16)
                elif kind == 1:
                    k_ref[:, h * HEAD_W:(h + 1) * HEAD_W] = (rope(yb) if with_q else yb).astype(BF16)
                else:
                    vt_ref[h, 0:HEAD_W, :] = yb.T.astype(BF16)
                    vt_ref[h, HEAD_W:VT_ROWS, :] = jnp.ones((VT_ROWS - HEAD_W, yb.shape[0]), BF16)


def _qkv(h, mod, ctx_row, w, tables, tm):
    b, l, d = h.shape
    with_q = tables is not None
    heads = w.shape[1] // HEAD_W // (3 if with_q else 2)
    kern = functools.partial(_qkv_kernel, heads=heads, with_q=with_q)
    in_specs = [pl.BlockSpec((None, tm, d), lambda b, i: (b, i, 0)), _mod_spec(d, ctx_row), _resident(w.shape)]
    args = [h, mod, w]
    out_specs = [pl.BlockSpec((None, tm, heads * HEAD_W), lambda b, i: (b, i, 0)),
                 pl.BlockSpec((None, heads, None, VT_ROWS, tm), lambda b, i: (b, 0, i, 0, 0))]
    out_shape = [jax.ShapeDtypeStruct((b, l, heads * HEAD_W), BF16),
                 jax.ShapeDtypeStruct((b, heads, l // tm, VT_ROWS, tm), BF16)]
    if with_q:
        in_specs += [pl.BlockSpec((tm, LANES), lambda b, i: (i, 0))] * 3
        args += list(tables)
        out_specs.insert(0, pl.BlockSpec((None, heads, HEAD_W, tm), lambda b, i: (b, 0, 0, i)))
        out_shape.insert(0, jax.ShapeDtypeStruct((b, heads, HEAD_W, l), BF16))
    outs = pl.pallas_call(
        kern,
        grid=(b, l // tm),
        in_specs=in_specs,
        out_specs=out_specs,
        out_shape=out_shape,
        compiler_params=_params(2),
        name="qkv",
    )(*args)
    return outs if with_q else (None, *outs)


def _rope_tables(n_lat):
    rows = n_lat // GRID_W
    row_pos = jnp.broadcast_to(jnp.arange(rows, dtype=F32)[:, None], (rows, GRID_W)).reshape(-1)
    col_pos = jnp.broadcast_to(jnp.arange(GRID_W, dtype=F32)[None, :], (rows, GRID_W)).reshape(-1)
    n_freq = SUB_HEAD // 4
    inv_freq = ROPE_THETA ** (-jnp.arange(n_freq, dtype=F32) / n_freq)
    ang_r = row_pos[:, None] * inv_freq
    ang_c = col_pos[:, None] * inv_freq
    ang = jnp.concatenate([ang_r, ang_r, ang_c, ang_c] * 2, axis=-1)
    first = (jnp.arange(LANES) % (2 * n_freq)) < n_freq
    sin = jnp.sin(ang)
    return jnp.cos(ang), jnp.where(first, -sin, 0.0), jnp.where(first, 0.0, sin)


def _attn_kernel(qt_ref, kc_ref, vtc_ref, kl_ref, vtl_ref, lam_ref, g_ref, o_ref, qs_ref, s_ref, mx_ref, m_ref, acc_ref,
                 *, tq, lambda_init):
    dim = lax.broadcasted_iota(jnp.int32, (HEAD_W, tq), 0)
    qt = qt_ref[...]
    zero = jnp.zeros_like(qt)
    qs_ref[:, 0:tq] = jnp.where(dim < SUB_HEAD, qt, zero)
    qs_ref[:, tq:2 * tq] = jnp.where(dim >= SUB_HEAD, qt, zero)
    m_ref[...] = jnp.full_like(m_ref, -jnp.inf)
    acc_ref[...] = jnp.zeros_like(acc_ref)
    n_chunks, _, tk = vtl_ref.shape
    n_ctx = kc_ref.shape[0]

    def stage(slot, k):
        st = jnp.dot(k, qs_ref[...], preferred_element_type=F32)
        s_ref[slot, 0:k.shape[0], :] = st
        mx_ref[slot] = jnp.max(st, axis=0, keepdims=True)

    def consume(slot, n, vt):
        m_prev = m_ref[...]
        m_new = jnp.maximum(m_prev, mx_ref[slot])
        alpha = jnp.exp2(m_prev - m_new)
        pt = jnp.exp2(s_ref[slot, 0:n, :] - m_new).astype(BF16)
        acc_ref[...] = alpha * acc_ref[...] + jnp.dot(vt, pt, preferred_element_type=F32)
        m_ref[...] = m_new

    def lat_k(j):
        return kl_ref[pl.ds(pl.multiple_of(j * tk, tk), tk), :]

    stage(1, kc_ref[...])
    stage(0, lat_k(0))
    consume(1, n_ctx, vtc_ref[0])

    def quad(jj, carry):
        j = 4 * jj + 1
        for t in range(4):
            stage((t + 1) % 2, lat_k(j + t))
            consume(t % 2, tk, vtl_ref[j + t - 1])
        return carry

    n_quads = (n_chunks - 1) // 4
    lax.fori_loop(0, n_quads, quad, 0)
    slot = 1
    for j in range(4 * n_quads + 1, n_chunks):
        stage(slot, lat_k(j))
        consume(1 - slot, tk, vtl_ref[j - 1])
        slot = 1 - slot
    consume(1 - slot, tk, vtl_ref[n_chunks - 1])

    lp = lam_ref[...]
    lam = (jnp.exp(jnp.sum(lp[0:1, :] * lp[1:2, :], axis=1, keepdims=True))
           - jnp.exp(jnp.sum(lp[2:3, :] * lp[3:4, :], axis=1, keepdims=True)) + lambda_init)
    ot = acc_ref[0:HEAD_W, :] / acc_ref[HEAD_W:HEAD_W + 1, :]
    od = ot[:, 0:tq] - lam * ot[:, tq:2 * tq]
    od = od * lax.rsqrt(jnp.mean(od * od, axis=0, keepdims=True) + EPS) * (g_ref[...] * (1.0 - lambda_init))
    o_ref[...] = od.T.astype(BF16)


def _attn(qt, k_ctx, vt_ctx, k_lat, vt_lat, lam_params, subln_g, lambda_init, tq):
    b, heads, _, n_lat = qt.shape
    n_ctx = k_ctx.shape[1]
    n_chunks, _, tk = vt_lat.shape[2:]
    assert n_ctx <= tk
    kern = functools.partial(_attn_kernel, tq=tq, lambda_init=lambda_init)
    vt_block = lambda a: pl.BlockSpec((None, None) + a.shape[2:], lambda b, h, i: (b, h, 0, 0, 0))
    return pl.pallas_call(
        kern,
        grid=(b, heads, n_lat // tq),
        in_specs=[pl.BlockSpec((None, None, HEAD_W, tq), lambda b, h, i: (b, h, 0, i)),
                  pl.BlockSpec((None, n_ctx, HEAD_W), lambda b, h, i: (b, 0, h)),
                  vt_block(vt_ctx),
                  pl.BlockSpec((None, n_lat, HEAD_W), lambda b, h, i: (b, 0, h)),
                  vt_block(vt_lat),
                  pl.BlockSpec(lam_params.shape, lambda b, h, i: (0, 0)),
                  pl.BlockSpec(subln_g.shape, lambda b, h, i: (0, 0))],
        out_specs=pl.BlockSpec((None, tq, HEAD_W), lambda b, h, i: (b, i, h)),
        out_shape=jax.ShapeDtypeStruct((b, n_lat, heads * HEAD_W), BF16),
        scratch_shapes=[pltpu.VMEM((HEAD_W, 2 * tq), BF16), pltpu.VMEM((2, tk, 2 * tq), F32),
                        pltpu.VMEM((2, 1, 2 * tq), F32), pltpu.VMEM((1, 2 * tq), F32),
                        pltpu.VMEM((VT_ROWS, 2 * tq), F32)],
        compiler_params=_params(3),
        name="attn",
    )(qt, k_ctx, vt_ctx, k_lat, vt_lat, lam_params, subln_g)


def _proj_kernel(o_ref, h_ref, mod_ref, w_ref, out_ref):
    out_ref[...] = h_ref[...] + mod_ref[2:3, :] * jnp.dot(o_ref[...], w_ref[...], preferred_element_type=F32)


def _proj(o, h, mod, w, tm):
    b, l, d = h.shape
    return pl.pallas_call(
        _proj_kernel,
        grid=(b, l // tm),
        in_specs=[pl.BlockSpec((None, tm, o.shape[2]), lambda b, i: (b, i, 0)),
                  pl.BlockSpec((None, tm, d), lambda b, i: (b, i, 0)),
                  _mod_spec(d, None), _resident(w.shape)],
        out_specs=pl.BlockSpec((None, tm, d), lambda b, i: (b, i, 0)),
        out_shape=jax.ShapeDtypeStruct(h.shape, F32),
        compiler_params=_params(2),
        name="proj",
    )(o, h, mod, w)


def _tile(l, target):
    return min(l, target)


def kernel(x, c, ctx, c_ctx, ada_w, ada_b, gm_w_in, gm_norm_g, gm_w_s, gm_b_s, gm_w_out, da_w_qkv, da_lambda_q1, da_lambda_k1, da_lambda_q2, da_lambda_k2, da_subln_g, da_w_out, ffn_w_up, ffn_conv_w, ffn_conv_b, ffn_w_down, final_norm_g):
    bsz, n_lat, d = x.shape
    depth = ada_w.shape[0]
    ffn_dim = ffn_w_down.shape[1]
    assert depth == 2 and bsz < ADA_ROWS and n_lat % GRID_W == 0
    ctx_row = bsz
    ffn_cw = 2 * LANES
    n_chunks = ffn_dim // ffn_cw
    assert n_chunks * ffn_cw == ffn_dim

    cond = jnp.zeros((ADA_ROWS, d), F32).at[:bsz].set(c).at[ctx_row].set(c_ctx)
    mod = _ada(cond, ada_w, ada_b).reshape(depth, ADA_ROWS, 6, d)

    def ffn_weights(i):
        w_up = ffn_w_up[i].astype(BF16).reshape(d, 2, n_chunks, ffn_cw).transpose(1, 2, 0, 3)
        cw = ffn_conv_w[i].reshape(CONV_W, 2, n_chunks, ffn_cw).transpose(1, 2, 0, 3)
        cb = ffn_conv_b[i].reshape(2, n_chunks, 1, ffn_cw)
        w_dn = ffn_w_down[i].astype(BF16).reshape(n_chunks, ffn_cw, d)
        return w_up, cw, cb, w_dn

    tm_lat = _tile(n_lat, 512)
    tm_ctx = _tile(ctx.shape[1], 512)

    gm = (gm_w_in[0].astype(BF16), gm_norm_g[0].reshape(1, -1), gm_w_s[0].astype(BF16),
          gm_b_s[0][:, :, None], gm_w_out[0].astype(BF16))
    fw = ffn_weights(0)
    h = _gmlp(x, mod[0], None, *gm, tm_lat)
    hc = _gmlp(ctx, mod[0], ctx_row, *gm, tm_ctx)
    h = _ffn(h, mod[0], None, *fw, None, tm_lat)
    hc = _ffn(hc, mod[0], ctx_row, *fw, None, tm_ctx)

    lambda_init = 0.8 - 0.6 * math.exp(-0.3 * 1)
    w_qkv = da_w_qkv[0].astype(BF16)
    da_width = w_qkv.shape[1] // 3
    qt, k_lat, vt_lat = _qkv(h, mod[1], None, w_qkv, _rope_tables(n_lat), tm_lat)
    _, k_ctx, vt_ctx = _qkv(hc, mod[1], ctx_row, w_qkv[:, da_width:], None, tm_ctx)
    lam_params = jnp.stack([da_lambda_q1[0], da_lambda_k1[0], da_lambda_q2[0], da_lambda_k2[0]])
    o = _attn(qt, k_ctx, vt_ctx, k_lat, vt_lat, lam_params, da_subln_g[0].reshape(-1, 1), lambda_init, 512)
    h = _proj(o, h, mod[1], da_w_out[0].astype(BF16), tm_lat)
    fw = ffn_weights(1)
    return _ffn(h, mod[1], None, *fw, final_norm_g.reshape(1, -1), tm_lat)
```

```python
import functools
import math

import jax
import jax.numpy as jnp
from jax import lax
from jax.experimental import pallas as pl
from jax.experimental.pallas import tpu as pltpu

EPS = 1e-6
GRID_W = 64
GM_GROUPS = 8
CHUNK = 128
SUB_HEAD = 64
HEAD_W = 2 * SUB_HEAD
VT_ROWS = HEAD_W + 16
ROPE_THETA = 10000.0
CONV_W = 3
HALO = 16
LANES = 128
ADA_ROWS = 16
PIPE_BODY = 4
MAX_GROWTH_LOG2 = 32.0
FFN_PIPE_BODY = 2
VMEM_LIMIT_BYTES = 56 * 1024 * 1024

F32 = jnp.float32
BF16 = jnp.bfloat16


def _rms(x):
    return x * lax.rsqrt(jnp.mean(x * x, axis=-1, keepdims=True) + EPS)


def _modulate(x, mod_ref, k):
    return _rms(x) * (1.0 + mod_ref[k + 1:k + 2, :]) + mod_ref[k:k + 1, :]


def _silu(x):
    return x * (1.0 / (1.0 + jnp.exp(-x)))


def _gelu_tanh(x):
    return 0.5 * x * (1.0 + jnp.tanh(math.sqrt(2.0 / math.pi) * (x + 0.044715 * (x * x * x))))


def _params(n_axes):
    return pltpu.CompilerParams(dimension_semantics=("arbitrary",) * n_axes,
                                vmem_limit_bytes=VMEM_LIMIT_BYTES)


def _resident(shape):
    nd = len(shape)
    return pl.BlockSpec(shape, lambda *_: (0,) * nd, pipeline_mode=pl.Buffered(1))


def _ada_kernel(cond_ref, w_ref, b_ref, out_ref):
    s = _silu(cond_ref[...])
    out_ref[...] = jnp.dot(s, w_ref[...], preferred_element_type=F32,
                           precision=lax.Precision.HIGHEST) + b_ref[...]


def _ada(cond, ada_w, ada_b):
    depth, d, n = ada_w.shape
    nb = 6 * LANES * 2
    return pl.pallas_call(
        _ada_kernel,
        grid=(depth, n // nb),
        in_specs=[pl.BlockSpec((ADA_ROWS, d), lambda l, j: (0, 0)),
                  pl.BlockSpec((None, d, nb), lambda l, j: (l, 0, j)),
                  pl.BlockSpec((None, 1, nb), lambda l, j: (l, 0, j))],
        out_specs=pl.BlockSpec((None, ADA_ROWS, nb), lambda l, j: (l, 0, j)),
        out_shape=jax.ShapeDtypeStruct((depth, ADA_ROWS, n), F32),
        compiler_params=_params(2),
        name="ada",
    )(cond, ada_w, ada_b.reshape(depth, 1, n))


def _mod_spec(d, ctx_row):
    if ctx_row is None:
        return pl.BlockSpec((None, 6, d), lambda b, i: (b, 0, 0))
    return pl.BlockSpec((None, 6, d), lambda b, i: (ctx_row, 0, 0))


def _gmlp_kernel(h_ref, mod_ref, win_ref, ng_ref, ws_ref, bs_ref, wout_ref, out_ref, vn_ref, uv_ref, *, tm, gw):
    x = h_ref[...]
    xl = _modulate(x, mod_ref, 0).astype(BF16)
    v = _gelu_tanh(jnp.dot(xl, win_ref[:, gw:], preferred_element_type=F32))
    vn_ref[...] = (_rms(v) * ng_ref[...]).astype(BF16)
    gd = gw // GM_GROUPS
    for g in range(GM_GROUPS):
        cols = slice(g * gd, (g + 1) * gd)
        u = _gelu_tanh(jnp.dot(xl, win_ref[:, cols], preferred_element_type=F32))
        for c in range(tm // CHUNK):
            rows = slice(c * CHUNK, (c + 1) * CHUNK)
            mix = jnp.dot(ws_ref[g], vn_ref[rows, cols], preferred_element_type=F32) + bs_ref[g]
            uv_ref[rows, cols] = (u[rows, :] * mix).astype(BF16)
    out = jnp.dot(uv_ref[...], wout_ref[...], preferred_element_type=F32)
    out_ref[...] = x + mod_ref[2:3, :] * out


def _gmlp(h, mod, ctx_row, w_in, norm_g, w_s, b_s, w_out, tm):
    b, l, d = h.shape
    gw = w_out.shape[0]
    kern = functools.partial(_gmlp_kernel, tm=tm, gw=gw)
    return pl.pallas_call(
        kern,
        grid=(b, l // tm),
        in_specs=[pl.BlockSpec((None, tm, d), lambda b, i: (b, i, 0)),
                  _mod_spec(d, ctx_row),
                  _resident(w_in.shape), _resident(norm_g.shape), _resident(w_s.shape),
                  _resident(b_s.shape), _resident(w_out.shape)],
        out_specs=pl.BlockSpec((None, tm, d), lambda b, i: (b, i, 0)),
        out_shape=jax.ShapeDtypeStruct(h.shape, F32),
        scratch_shapes=[pltpu.VMEM((tm, gw), BF16), pltpu.VMEM((tm, gw), BF16)],
        compiler_params=_params(2),
        name="gmlp",
    )(h, mod, w_in, norm_g, w_s, b_s, w_out)


def _ffn_kernel(*refs, tm, n_chunks, final):
    if final:
        h_ref, hp_ref, hn_ref, mod_ref, wup_ref, cw_ref, cb_ref, wdn_ref, fg_ref, out_ref, xm_ref, acc_ref, z_ref = refs
    else:
        h_ref, hp_ref, hn_ref, mod_ref, wup_ref, cw_ref, cb_ref, wdn_ref, out_ref, xm_ref, acc_ref, z_ref = refs
    i = pl.program_id(1)
    last = pl.num_programs(1) - 1
    x = h_ref[...]
    xm_ref[0:HALO, :] = jnp.where(i == 0, 0.0, _modulate(hp_ref[...], mod_ref, 3)).astype(BF16)
    xm_ref[HALO:HALO + tm, :] = _modulate(x, mod_ref, 3).astype(BF16)
    xm_ref[HALO + tm:, :] = jnp.where(i == last, 0.0, _modulate(hn_ref[...], mod_ref, 3)).astype(BF16)
    acc_ref[...] = jnp.zeros_like(acc_ref)

    def stage(slot, c):
        xm = xm_ref[...]
        for s in range(2):
            z_ref[slot, s] = jnp.dot(xm, wup_ref[s, c], preferred_element_type=F32)

    def consume(slot, c):
        def conv_half(s):
            cw = cw_ref[s, c]
            taps = [z_ref[slot, s, HALO - 1 + j:HALO - 1 + j + tm, :] * cw[j:j + 1, :] for j in range(CONV_W)]
            return taps[0] + taps[1] + taps[2] + cb_ref[s, c]

        act = (_silu(conv_half(0)) * conv_half(1)).astype(BF16)
        acc_ref[...] += jnp.dot(act, wdn_ref[c], preferred_element_type=F32)

    def body(jj, carry):
        c = FFN_PIPE_BODY * jj
        for t in range(FFN_PIPE_BODY):
            stage((t + 1) % 2, c + t + 1)
            consume(t % 2, c + t)
        return carry

    n_body = (n_chunks - 1) // FFN_PIPE_BODY
    stage(0, 0)
    lax.fori_loop(0, n_body, body, 0)
    slot = 0
    for c in range(FFN_PIPE_BODY * n_body, n_chunks - 1):
        stage(1 - slot, c + 1)
        consume(slot, c)
        slot = 1 - slot
    consume(slot, n_chunks - 1)
    y = x + mod_ref[5:6, :] * acc_ref[...]
    if final:
        y = _rms(y) * fg_ref[...]
    out_ref[...] = y


def _ffn(h, mod, ctx_row, w_up, conv_w, conv_b, w_down, final_g, tm):
    b, l, d = h.shape
    n_chunks = w_down.shape[0]
    nh = l // HALO
    per = tm // HALO
    final = final_g is not None
    kern = functools.partial(_ffn_kernel, tm=tm, n_chunks=n_chunks, final=final)
    in_specs = [pl.BlockSpec((None, tm, d), lambda b, i: (b, i, 0)),
                pl.BlockSpec((None, HALO, d), lambda b, i: (b, jnp.maximum(i * per - 1, 0), 0)),
                pl.BlockSpec((None, HALO, d), lambda b, i: (b, jnp.minimum((i + 1) * per, nh - 1), 0)),
                _mod_spec(d, ctx_row),
                _resident(w_up.shape), _resident(conv_w.shape), _resident(conv_b.shape), _resident(w_down.shape)]
    args = [h, h, h, mod, w_up, conv_w, conv_b, w_down]
    if final:
        in_specs.append(_resident(final_g.shape))
        args.append(final_g)
    return pl.pallas_call(
        kern,
        grid=(b, l // tm),
        in_specs=in_specs,
        out_specs=pl.BlockSpec((None, tm, d), lambda b, i: (b, i, 0)),
        out_shape=jax.ShapeDtypeStruct(h.shape, F32),
        scratch_shapes=[pltpu.VMEM((tm + 2 * HALO, d), BF16), pltpu.VMEM((tm, d), F32),
                        pltpu.VMEM((2, 2, tm + 2 * HALO, w_up.shape[3]), F32)],
        compiler_params=_params(2),
        name="ffn",
    )(*args)


def _qkv_kernel(*refs, heads, with_q):
    if with_q:
        h_ref, mod_ref, w_ref, cos_ref, sa_ref, sb_ref, qt_ref, k_ref, vt_ref = refs
    else:
        h_ref, mod_ref, w_ref, k_ref, vt_ref = refs
    xl = _modulate(h_ref[...], mod_ref, 0).astype(BF16)

    def rope(yb):
        return (yb * cos_ref[...] + pltpu.roll(yb, LANES - SUB_HEAD // 4, 1) * sa_ref[...]
                + pltpu.roll(yb, SUB_HEAD // 4, 1) * sb_ref[...])

    group = 4
    for part in range(3 if with_q else 2):
        kind = part if with_q else part + 1
        for h0 in range(0, heads, group):
            c0 = (part * heads + h0) * HEAD_W
            y = jnp.dot(xl, w_ref[:, c0:c0 + group * HEAD_W], preferred_element_type=F32)
            for hh in range(group):
                h = h0 + hh
                yb = y[:, hh * HEAD_W:(hh + 1) * HEAD_W]
                if kind == 0:
                    qt_ref[h] = (rope(yb) * (SUB_HEAD ** -0.5 * math.log2(math.e))).T.astype(BF16)
                elif kind == 1:
                    k_ref[h] = (rope(yb) if with_q else yb).astype(BF16)
                else:
                    vt_ref[h, 0:HEAD_W, :] = yb.T.astype(BF16)
                    vt_ref[h, HEAD_W:VT_ROWS, :] = jnp.ones((VT_ROWS - HEAD_W, yb.shape[0]), BF16)


def _qkv(h, mod, ctx_row, w, tables, tm):
    b, l, d = h.shape
    with_q = tables is not None
    heads = w.shape[1] // HEAD_W // (3 if with_q else 2)
    kern = functools.partial(_qkv_kernel, heads=heads, with_q=with_q)
    in_specs = [pl.BlockSpec((None, tm, d), lambda b, i: (b, i, 0)), _mod_spec(d, ctx_row), _resident(w.shape)]
    args = [h, mod, w]
    out_specs = [pl.BlockSpec((None, heads, tm, HEAD_W), lambda b, i: (b, 0, i, 0)),
                 pl.BlockSpec((None, heads, None, VT_ROWS, tm), lambda b, i: (b, 0, i, 0, 0))]
    out_shape = [jax.ShapeDtypeStruct((b, heads, l, HEAD_W), BF16),
                 jax.ShapeDtypeStruct((b, heads, l // tm, VT_ROWS, tm), BF16)]
    if with_q:
        in_specs += [pl.BlockSpec((tm, LANES), lambda b, i: (i, 0))] * 3
        args += list(tables)
        out_specs.insert(0, pl.BlockSpec((None, heads, HEAD_W, tm), lambda b, i: (b, 0, 0, i)))
        out_shape.insert(0, jax.ShapeDtypeStruct((b, heads, HEAD_W, l), BF16))
    outs = pl.pallas_call(
        kern,
        grid=(b, l // tm),
        in_specs=in_specs,
        out_specs=out_specs,
        out_shape=out_shape,
        compiler_params=_params(2),
        name="qkv",
    )(*args)
    return outs if with_q else (None, *outs)


def _rope_tables(n_lat):
    rows = n_lat // GRID_W
    row_pos = jnp.broadcast_to(jnp.arange(rows, dtype=F32)[:, None], (rows, GRID_W)).reshape(-1)
    col_pos = jnp.broadcast_to(jnp.arange(GRID_W, dtype=F32)[None, :], (rows, GRID_W)).reshape(-1)
    n_freq = SUB_HEAD // 4
    inv_freq = ROPE_THETA ** (-jnp.arange(n_freq, dtype=F32) / n_freq)
    ang_r = row_pos[:, None] * inv_freq
    ang_c = col_pos[:, None] * inv_freq
    ang = jnp.concatenate([ang_r, ang_r, ang_c, ang_c] * 2, axis=-1)
    first = (jnp.arange(LANES) % (2 * n_freq)) < n_freq
    sin = jnp.sin(ang)
    return jnp.cos(ang), jnp.where(first, -sin, 0.0), jnp.where(first, 0.0, sin)


def _attn_kernel(qt_ref, kc_ref, vtc_ref, kl_ref, vtl_ref, lam_ref, g_ref, o_ref,
                 qs_ref, s_ref, mx_ref, m_ref, acc_ref, grow_ref, *, tq, lambda_init):
    dim = lax.broadcasted_iota(jnp.int32, (HEAD_W, tq), 0)
    qt = qt_ref[...]
    zero = jnp.zeros_like(qt)
    qs_ref[:, 0:tq] = jnp.where(dim < SUB_HEAD, qt, zero)
    qs_ref[:, tq:2 * tq] = jnp.where(dim >= SUB_HEAD, qt, zero)
    n_chunks, _, tk = vtl_ref.shape
    n_ctx = kc_ref.shape[0]

    def lat_k(j):
        return kl_ref[pl.ds(pl.multiple_of(j * tk, tk), tk), :]

    def finish():
        lp = lam_ref[...]
        lam = (jnp.exp(jnp.sum(lp[0:1, :] * lp[1:2, :], axis=1, keepdims=True))
               - jnp.exp(jnp.sum(lp[2:3, :] * lp[3:4, :], axis=1, keepdims=True)) + lambda_init)
        ot = acc_ref[0:HEAD_W, :] / acc_ref[HEAD_W:HEAD_W + 1, :]
        od = ot[:, 0:tq] - lam * ot[:, tq:2 * tq]
        od = od * lax.rsqrt(jnp.mean(od * od, axis=0, keepdims=True) + EPS) * (g_ref[...] * (1.0 - lambda_init))
        o_ref[...] = od.T.astype(BF16)

    def single_pass_block(j):
        st = jnp.dot(lat_k(j), qs_ref[...], preferred_element_type=F32)
        m_prev = m_ref[...]
        pt = jnp.exp2(st - m_prev).astype(BF16)
        mx = jnp.max(st, axis=0, keepdims=True)
        m_new = jnp.maximum(m_prev, mx)
        grow_ref[...] = jnp.maximum(grow_ref[...], mx - m_prev)
        pv = jnp.dot(vtl_ref[j], pt, preferred_element_type=F32)
        acc_ref[...] = (acc_ref[...] + pv) * jnp.exp2(m_prev - m_new)
        m_ref[...] = m_new

    st = jnp.dot(kc_ref[...], qs_ref[...], preferred_element_type=F32)
    m_ref[...] = jnp.max(st, axis=0, keepdims=True)
    acc_ref[...] = jnp.dot(vtc_ref[0], jnp.exp2(st - m_ref[...]).astype(BF16), preferred_element_type=F32)
    grow_ref[...] = jnp.zeros_like(grow_ref)

    def single_pass_body(jj, carry):
        for t in range(PIPE_BODY):
            single_pass_block(PIPE_BODY * jj + t)
        return carry

    n_body = n_chunks // PIPE_BODY
    lax.fori_loop(0, n_body, single_pass_body, 0)
    for j in range(PIPE_BODY * n_body, n_chunks):
        single_pass_block(j)
    finish()

    @pl.when(jnp.max(grow_ref[...]) > MAX_GROWTH_LOG2)
    def _():
        m_ref[...] = jnp.full_like(m_ref, -jnp.inf)
        acc_ref[...] = jnp.zeros_like(acc_ref)

        def stage(slot, k):
            st = jnp.dot(k, qs_ref[...], preferred_element_type=F32)
            s_ref[slot, 0:k.shape[0], :] = st
            mx_ref[slot] = jnp.max(st, axis=0, keepdims=True)

        def consume(slot, n, vt):
            m_prev = m_ref[...]
            m_new = jnp.maximum(m_prev, mx_ref[slot])
            alpha = jnp.exp2(m_prev - m_new)
            pt = jnp.exp2(s_ref[slot, 0:n, :] - m_new).astype(BF16)
            acc_ref[...] = alpha * acc_ref[...] + jnp.dot(vt, pt, preferred_element_type=F32)
            m_ref[...] = m_new

        stage(1, kc_ref[...])
        stage(0, lat_k(0))
        consume(1, n_ctx, vtc_ref[0])

        def pair(jj, carry):
            j = 2 * jj + 1
            stage(1, lat_k(j))
            consume(0, tk, vtl_ref[j - 1])
            stage(0, lat_k(j + 1))
            consume(1, tk, vtl_ref[j])
            return carry

        n_pairs = (n_chunks - 1) // 2
        lax.fori_loop(0, n_pairs, pair, 0)
        slot = 1
        for j in range(2 * n_pairs + 1, n_chunks):
            stage(slot, lat_k(j))
            consume(1 - slot, tk, vtl_ref[j - 1])
            slot = 1 - slot
        consume(1 - slot, tk, vtl_ref[n_chunks - 1])
        finish()


def _attn(qt, k_ctx, vt_ctx, k_lat, vt_lat, lam_params, subln_g, lambda_init, tq):
    b, heads, _, n_lat = qt.shape
    n_ctx = k_ctx.shape[2]
    n_chunks, _, tk = vt_lat.shape[2:]
    assert n_ctx <= tk
    kern = functools.partial(_attn_kernel, tq=tq, lambda_init=lambda_init)
    vt_block = lambda a: pl.BlockSpec((None, None) + a.shape[2:], lambda b, h, i: (b, h, 0, 0, 0))
    return pl.pallas_call(
        kern,
        grid=(b, heads, n_lat // tq),
        in_specs=[pl.BlockSpec((None, None, HEAD_W, tq), lambda b, h, i: (b, h, 0, i)),
                  pl.BlockSpec((None, None, n_ctx, HEAD_W), lambda b, h, i: (b, h, 0, 0)),
                  vt_block(vt_ctx),
                  pl.BlockSpec((None, None, n_lat, HEAD_W), lambda b, h, i: (b, h, 0, 0)),
                  vt_block(vt_lat),
                  pl.BlockSpec(lam_params.shape, lambda b, h, i: (0, 0)),
                  pl.BlockSpec(subln_g.shape, lambda b, h, i: (0, 0))],
        out_specs=pl.BlockSpec((None, tq, HEAD_W), lambda b, h, i: (b, i, h)),
        out_shape=jax.ShapeDtypeStruct((b, n_lat, heads * HEAD_W), BF16),
        scratch_shapes=[pltpu.VMEM((HEAD_W, 2 * tq), BF16), pltpu.VMEM((2, tk, 2 * tq), F32),
                        pltpu.VMEM((2, 1, 2 * tq), F32), pltpu.VMEM((1, 2 * tq), F32),
                        pltpu.VMEM((VT_ROWS, 2 * tq), F32), pltpu.VMEM((1, 2 * tq), F32)],
        compiler_params=_params(3),
        name="attn",
    )(qt, k_ctx, vt_ctx, k_lat, vt_lat, lam_params, subln_g)


def _proj_kernel(o_ref, h_ref, mod_ref, w_ref, out_ref):
    out_ref[...] = h_ref[...] + mod_ref[2:3, :] * jnp.dot(o_ref[...], w_ref[...], preferred_element_type=F32)


def _proj(o, h, mod, w, tm):
    b, l, d = h.shape
    return pl.pallas_call(
        _proj_kernel,
        grid=(b, l // tm),
        in_specs=[pl.BlockSpec((None, tm, o.shape[2]), lambda b, i: (b, i, 0)),
                  pl.BlockSpec((None, tm, d), lambda b, i: (b, i, 0)),
                  _mod_spec(d, None), _resident(w.shape)],
        out_specs=pl.BlockSpec((None, tm, d), lambda b, i: (b, i, 0)),
        out_shape=jax.ShapeDtypeStruct(h.shape, F32),
        compiler_params=_params(2),
        name="proj",
    )(o, h, mod, w)


def _tile(l, target):
    return min(l, target)


def kernel(x, c, ctx, c_ctx, ada_w, ada_b, gm_w_in, gm_norm_g, gm_w_s, gm_b_s, gm_w_out, da_w_qkv, da_lambda_q1, da_lambda_k1, da_lambda_q2, da_lambda_k2, da_subln_g, da_w_out, ffn_w_up, ffn_conv_w, ffn_conv_b, ffn_w_down, final_norm_g):
    bsz, n_lat, d = x.shape
    depth = ada_w.shape[0]
    ffn_dim = ffn_w_down.shape[1]
    assert depth == 2 and bsz < ADA_ROWS and n_lat % GRID_W == 0
    ctx_row = bsz
    ffn_cw = 2 * LANES
    n_chunks = ffn_dim // ffn_cw
    assert n_chunks * ffn_cw == ffn_dim

    cond = jnp.zeros((ADA_ROWS, d), F32).at[:bsz].set(c).at[ctx_row].set(c_ctx)
    mod = _ada(cond, ada_w, ada_b).reshape(depth, ADA_ROWS, 6, d)

    def ffn_weights(i):
        w_up = ffn_w_up[i].astype(BF16).reshape(d, 2, n_chunks, ffn_cw).transpose(1, 2, 0, 3)
        cw = ffn_conv_w[i].reshape(CONV_W, 2, n_chunks, ffn_cw).transpose(1, 2, 0, 3)
        cb = ffn_conv_b[i].reshape(2, n_chunks, 1, ffn_cw)
        w_dn = ffn_w_down[i].astype(BF16).reshape(n_chunks, ffn_cw, d)
        return w_up, cw, cb, w_dn

    tm_lat = _tile(n_lat, 512)
    tm_ctx = _tile(ctx.shape[1], 512)

    gm = (gm_w_in[0].astype(BF16), gm_norm_g[0].reshape(1, -1), gm_w_s[0].astype(BF16),
          gm_b_s[0][:, :, None], gm_w_out[0].astype(BF16))
    fw = ffn_weights(0)
    h = _gmlp(x, mod[0], None, *gm, tm_lat)
    hc = _gmlp(ctx, mod[0], ctx_row, *gm, tm_ctx)
    h = _ffn(h, mod[0], None, *fw, None, tm_lat)
    hc = _ffn(hc, mod[0], ctx_row, *fw, None, tm_ctx)

    lambda_init = 0.8 - 0.6 * math.exp(-0.3 * 1)
    w_qkv = da_w_qkv[0].astype(BF16)
    da_width = w_qkv.shape[1] // 3
    qt, k_lat, vt_lat = _qkv(h, mod[1], None, w_qkv, _rope_tables(n_lat), tm_lat)
    _, k_ctx, vt_ctx = _qkv(hc, mod[1], ctx_row, w_qkv[:, da_width:], None, tm_ctx)
    lam_params = jnp.stack([da_lambda_q1[0], da_lambda_k1[0], da_lambda_q2[0], da_lambda_k2[0]])
    o = _attn(qt, k_ctx, vt_ctx, k_lat, vt_lat, lam_params, da_subln_g[0].reshape(-1, 1), lambda_init, 512)
    h = _proj(o, h, mod[1], da_w_out[0].astype(BF16), tm_lat)
    fw = ffn_weights(1)
    return _ffn(h, mod[1], None, *fw, final_norm_g.reshape(1, -1), tm_lat)
```

```python
import functools
import math

import jax
import jax.numpy as jnp
from jax import lax
from jax.experimental import pallas as pl
from jax.experimental.pallas import tpu as pltpu

EPS = 1e-6
GRID_W = 64
GM_GROUPS = 8
CHUNK = 128
SUB_HEAD = 64
HEAD_W = 2 * SUB_HEAD
VT_ROWS = HEAD_W + 16
ROPE_THETA = 10000.0
CONV_W = 3
HALO = 16
LANES = 128
ADA_ROWS = 16
PIPE_BODY = 16
MAX_GROWTH_LOG2 = 32.0
FFN_PIPE_BODY = 2
VMEM_LIMIT_BYTES = 56 * 1024 * 1024

F32 = jnp.float32
BF16 = jnp.bfloat16


def _rms(x):
    return x * lax.rsqrt(jnp.mean(x * x, axis=-1, keepdims=True) + EPS)


def _modulate(x, mod_ref, k):
    return _rms(x) * (1.0 + mod_ref[k + 1:k + 2, :]) + mod_ref[k:k + 1, :]


def _silu(x):
    return x * (1.0 / (1.0 + jnp.exp(-x)))


def _gelu_tanh(x):
    return 0.5 * x * (1.0 + jnp.tanh(math.sqrt(2.0 / math.pi) * (x + 0.044715 * (x * x * x))))


def _params(n_axes):
    return pltpu.CompilerParams(dimension_semantics=("arbitrary",) * n_axes,
                                vmem_limit_bytes=VMEM_LIMIT_BYTES)


def _resident(shape):
    nd = len(shape)
    return pl.BlockSpec(shape, lambda *_: (0,) * nd, pipeline_mode=pl.Buffered(1))


def _ada_kernel(cond_ref, w_ref, b_ref, out_ref):
    s = _silu(cond_ref[...])
    out_ref[...] = jnp.dot(s, w_ref[...], preferred_element_type=F32,
                           precision=lax.Precision.HIGHEST) + b_ref[...]


def _ada(cond, ada_w, ada_b):
    depth, d, n = ada_w.shape
    nb = 6 * LANES * 2
    return pl.pallas_call(
        _ada_kernel,
        grid=(depth, n // nb),
        in_specs=[pl.BlockSpec((ADA_ROWS, d), lambda l, j: (0, 0)),
                  pl.BlockSpec((None, d, nb), lambda l, j: (l, 0, j)),
                  pl.BlockSpec((None, 1, nb), lambda l, j: (l, 0, j))],
        out_specs=pl.BlockSpec((None, ADA_ROWS, nb), lambda l, j: (l, 0, j)),
        out_shape=jax.ShapeDtypeStruct((depth, ADA_ROWS, n), F32),
        compiler_params=_params(2),
        name="ada",
    )(cond, ada_w, ada_b.reshape(depth, 1, n))


def _mod_spec(d, ctx_row):
    if ctx_row is None:
        return pl.BlockSpec((None, 6, d), lambda b, i: (b, 0, 0))
    return pl.BlockSpec((None, 6, d), lambda b, i: (ctx_row, 0, 0))


def _gmlp_kernel(h_ref, mod_ref, win_ref, ng_ref, ws_ref, bs_ref, wout_ref, out_ref, vn_ref, uv_ref, *, tm, gw):
    x = h_ref[...]
    xl = _modulate(x, mod_ref, 0).astype(BF16)
    v = _gelu_tanh(jnp.dot(xl, win_ref[:, gw:], preferred_element_type=F32))
    vn_ref[...] = (_rms(v) * ng_ref[...]).astype(BF16)
    gd = gw // GM_GROUPS
    for g in range(GM_GROUPS):
        cols = slice(g * gd, (g + 1) * gd)
        u = _gelu_tanh(jnp.dot(xl, win_ref[:, cols], preferred_element_type=F32))
        for c in range(tm // CHUNK):
            rows = slice(c * CHUNK, (c + 1) * CHUNK)
            mix = jnp.dot(ws_ref[g], vn_ref[rows, cols], preferred_element_type=F32) + bs_ref[g]
            uv_ref[rows, cols] = (u[rows, :] * mix).astype(BF16)
    out = jnp.dot(uv_ref[...], wout_ref[...], preferred_element_type=F32)
    out_ref[...] = x + mod_ref[2:3, :] * out


def _gmlp(h, mod, ctx_row, w_in, norm_g, w_s, b_s, w_out, tm):
    b, l, d = h.shape
    gw = w_out.shape[0]
    kern = functools.partial(_gmlp_kernel, tm=tm, gw=gw)
    return pl.pallas_call(
        kern,
        grid=(b, l // tm),
        in_specs=[pl.BlockSpec((None, tm, d), lambda b, i: (b, i, 0)),
                  _mod_spec(d, ctx_row),
                  _resident(w_in.shape), _resident(norm_g.shape), _resident(w_s.shape),
                  _resident(b_s.shape), _resident(w_out.shape)],
        out_specs=pl.BlockSpec((None, tm, d), lambda b, i: (b, i, 0)),
        out_shape=jax.ShapeDtypeStruct(h.shape, F32),
        scratch_shapes=[pltpu.VMEM((tm, gw), BF16), pltpu.VMEM((tm, gw), BF16)],
        compiler_params=_params(2),
        name="gmlp",
    )(h, mod, w_in, norm_g, w_s, b_s, w_out)


def _ffn_kernel(*refs, tm, n_chunks, final):
    if final:
        h_ref, hp_ref, hn_ref, mod_ref, wup_ref, cw_ref, cb_ref, wdn_ref, fg_ref, out_ref, xm_ref, acc_ref, z_ref = refs
    else:
        h_ref, hp_ref, hn_ref, mod_ref, wup_ref, cw_ref, cb_ref, wdn_ref, out_ref, xm_ref, acc_ref, z_ref = refs
    i = pl.program_id(1)
    last = pl.num_programs(1) - 1
    x = h_ref[...]
    xm_ref[0:HALO, :] = jnp.where(i == 0, 0.0, _modulate(hp_ref[...], mod_ref, 3)).astype(BF16)
    xm_ref[HALO:HALO + tm, :] = _modulate(x, mod_ref, 3).astype(BF16)
    xm_ref[HALO + tm:, :] = jnp.where(i == last, 0.0, _modulate(hn_ref[...], mod_ref, 3)).astype(BF16)
    acc_ref[...] = jnp.zeros_like(acc_ref)

    def stage(slot, c):
        xm = xm_ref[...]
        for s in range(2):
            z_ref[slot, s] = jnp.dot(xm, wup_ref[s, c], preferred_element_type=F32)

    def consume(slot, c):
        def conv_half(s):
            cw = cw_ref[s, c]
            taps = [z_ref[slot, s, HALO - 1 + j:HALO - 1 + j + tm, :] * cw[j:j + 1, :] for j in range(CONV_W)]
            return taps[0] + taps[1] + taps[2] + cb_ref[s, c]

        act = (_silu(conv_half(0)) * conv_half(1)).astype(BF16)
        acc_ref[...] += jnp.dot(act, wdn_ref[c], preferred_element_type=F32)

    def body(jj, carry):
        c = FFN_PIPE_BODY * jj
        for t in range(FFN_PIPE_BODY):
            stage((t + 1) % 2, c + t + 1)
            consume(t % 2, c + t)
        return carry

    n_body = (n_chunks - 1) // FFN_PIPE_BODY
    stage(0, 0)
    lax.fori_loop(0, n_body, body, 0)
    slot = 0
    for c in range(FFN_PIPE_BODY * n_body, n_chunks - 1):
        stage(1 - slot, c + 1)
        consume(slot, c)
        slot = 1 - slot
    consume(slot, n_chunks - 1)
    y = x + mod_ref[5:6, :] * acc_ref[...]
    if final:
        y = _rms(y) * fg_ref[...]
    out_ref[...] = y


def _ffn(h, mod, ctx_row, w_up, conv_w, conv_b, w_down, final_g, tm):
    b, l, d = h.shape
    n_chunks = w_down.shape[0]
    nh = l // HALO
    per = tm // HALO
    final = final_g is not None
    kern = functools.partial(_ffn_kernel, tm=tm, n_chunks=n_chunks, final=final)
    in_specs = [pl.BlockSpec((None, tm, d), lambda b, i: (b, i, 0)),
                pl.BlockSpec((None, HALO, d), lambda b, i: (b, jnp.maximum(i * per - 1, 0), 0)),
                pl.BlockSpec((None, HALO, d), lambda b, i: (b, jnp.minimum((i + 1) * per, nh - 1), 0)),
                _mod_spec(d, ctx_row),
                _resident(w_up.shape), _resident(conv_w.shape), _resident(conv_b.shape), _resident(w_down.shape)]
    args = [h, h, h, mod, w_up, conv_w, conv_b, w_down]
    if final:
        in_specs.append(_resident(final_g.shape))
        args.append(final_g)
    return pl.pallas_call(
        kern,
        grid=(b, l // tm),
        in_specs=in_specs,
        out_specs=pl.BlockSpec((None, tm, d), lambda b, i: (b, i, 0)),
        out_shape=jax.ShapeDtypeStruct(h.shape, F32),
        scratch_shapes=[pltpu.VMEM((tm + 2 * HALO, d), BF16), pltpu.VMEM((tm, d), F32),
                        pltpu.VMEM((2, 2, tm + 2 * HALO, w_up.shape[3]), F32)],
        compiler_params=_params(2),
        name="ffn",
    )(*args)


def _qkv_kernel(*refs, heads, with_q):
    if with_q:
        h_ref, mod_ref, w_ref, cos_ref, sa_ref, sb_ref, qt_ref, k_ref, vt_ref = refs
    else:
        h_ref, mod_ref, w_ref, k_ref, vt_ref = refs
    xl = _modulate(h_ref[...], mod_ref, 0).astype(BF16)

    def rope(yb):
        return (yb * cos_ref[...] + pltpu.roll(yb, LANES - SUB_HEAD // 4, 1) * sa_ref[...]
                + pltpu.roll(yb, SUB_HEAD // 4, 1) * sb_ref[...])

    group = 4
    for part in range(3 if with_q else 2):
        kind = part if with_q else part + 1
        for h0 in range(0, heads, group):
            c0 = (part * heads + h0) * HEAD_W
            y = jnp.dot(xl, w_ref[:, c0:c0 + group * HEAD_W], preferred_element_type=F32)
            for hh in range(group):
                h = h0 + hh
                yb = y[:, hh * HEAD_W:(hh + 1) * HEAD_W]
                if kind == 0:
                    qt_ref[h] = (rope(yb) * (SUB_HEAD ** -0.5 * math.log2(math.e))).T.astype(BF16)
                elif kind == 1:
                    k_ref[h] = (rope(yb) if with_q else yb).astype(BF16)
                else:
                    vt_ref[h, 0:HEAD_W, :] = yb.T.astype(BF16)
                    vt_ref[h, HEAD_W:VT_ROWS, :] = jnp.ones((VT_ROWS - HEAD_W, yb.shape[0]), BF16)


def _qkv(h, mod, ctx_row, w, tables, tm):
    b, l, d = h.shape
    with_q = tables is not None
    heads = w.shape[1] // HEAD_W // (3 if with_q else 2)
    kern = functools.partial(_qkv_kernel, heads=heads, with_q=with_q)
    in_specs = [pl.BlockSpec((None, tm, d), lambda b, i: (b, i, 0)), _mod_spec(d, ctx_row), _resident(w.shape)]
    args = [h, mod, w]
    out_specs = [pl.BlockSpec((None, heads, tm, HEAD_W), lambda b, i: (b, 0, i, 0)),
                 pl.BlockSpec((None, heads, None, VT_ROWS, tm), lambda b, i: (b, 0, i, 0, 0))]
    out_shape = [jax.ShapeDtypeStruct((b, heads, l, HEAD_W), BF16),
                 jax.ShapeDtypeStruct((b, heads, l // tm, VT_ROWS, tm), BF16)]
    if with_q:
        in_specs += [pl.BlockSpec((tm, LANES), lambda b, i: (i, 0))] * 3
        args += list(tables)
        out_specs.insert(0, pl.BlockSpec((None, heads, HEAD_W, tm), lambda b, i: (b, 0, 0, i)))
        out_shape.insert(0, jax.ShapeDtypeStruct((b, heads, HEAD_W, l), BF16))
    outs = pl.pallas_call(
        kern,
        grid=(b, l // tm),
        in_specs=in_specs,
        out_specs=out_specs,
        out_shape=out_shape,
        compiler_params=_params(2),
        name="qkv",
    )(*args)
    return outs if with_q else (None, *outs)


def _rope_tables(n_lat):
    rows = n_lat // GRID_W
    row_pos = jnp.broadcast_to(jnp.arange(rows, dtype=F32)[:, None], (rows, GRID_W)).reshape(-1)
    col_pos = jnp.broadcast_to(jnp.arange(GRID_W, dtype=F32)[None, :], (rows, GRID_W)).reshape(-1)
    n_freq = SUB_HEAD // 4
    inv_freq = ROPE_THETA ** (-jnp.arange(n_freq, dtype=F32) / n_freq)
    ang_r = row_pos[:, None] * inv_freq
    ang_c = col_pos[:, None] * inv_freq
    ang = jnp.concatenate([ang_r, ang_r, ang_c, ang_c] * 2, axis=-1)
    first = (jnp.arange(LANES) % (2 * n_freq)) < n_freq
    sin = jnp.sin(ang)
    return jnp.cos(ang), jnp.where(first, -sin, 0.0), jnp.where(first, 0.0, sin)


def _attn_kernel(qt_ref, kc_ref, vtc_ref, kl_ref, vtl_ref, lam_ref, g_ref, o_ref,
                 qs_ref, s_ref, mx_ref, m_ref, acc_ref, grow_ref, *, tq, lambda_init):
    dim = lax.broadcasted_iota(jnp.int32, (HEAD_W, tq), 0)
    qt = qt_ref[...]
    zero = jnp.zeros_like(qt)
    qs_ref[:, 0:tq] = jnp.where(dim < SUB_HEAD, qt, zero)
    qs_ref[:, tq:2 * tq] = jnp.where(dim >= SUB_HEAD, qt, zero)
    n_chunks, _, tk = vtl_ref.shape
    n_ctx = kc_ref.shape[0]

    def lat_k(j):
        return kl_ref[pl.ds(pl.multiple_of(j * tk, tk), tk), :]

    def finish():
        lp = lam_ref[...]
        lam = (jnp.exp(jnp.sum(lp[0:1, :] * lp[1:2, :], axis=1, keepdims=True))
               - jnp.exp(jnp.sum(lp[2:3, :] * lp[3:4, :], axis=1, keepdims=True)) + lambda_init)
        ot = acc_ref[0:HEAD_W, :] / acc_ref[HEAD_W:HEAD_W + 1, :]
        od = ot[:, 0:tq] - lam * ot[:, tq:2 * tq]
        od = od * lax.rsqrt(jnp.mean(od * od, axis=0, keepdims=True) + EPS) * (g_ref[...] * (1.0 - lambda_init))
        o_ref[...] = od.T.astype(BF16)

    def single_pass_block(j):
        st = jnp.dot(lat_k(j), qs_ref[...], preferred_element_type=F32)
        m_prev = m_ref[...]
        pt = jnp.exp2(st - m_prev).astype(BF16)
        mx = jnp.max(st, axis=0, keepdims=True)
        m_new = jnp.maximum(m_prev, mx)
        grow_ref[...] = jnp.maximum(grow_ref[...], mx - m_prev)
        pv = jnp.dot(vtl_ref[j], pt, preferred_element_type=F32)
        acc_ref[...] = (acc_ref[...] + pv) * jnp.exp2(m_prev - m_new)
        m_ref[...] = m_new

    st = jnp.dot(kc_ref[...], qs_ref[...], preferred_element_type=F32)
    m_ref[...] = jnp.max(st, axis=0, keepdims=True)
    acc_ref[...] = jnp.dot(vtc_ref[0], jnp.exp2(st - m_ref[...]).astype(BF16), preferred_element_type=F32)
    grow_ref[...] = jnp.zeros_like(grow_ref)

    def single_pass_body(jj, carry):
        for t in range(PIPE_BODY):
            single_pass_block(PIPE_BODY * jj + t)
        return carry

    n_body = n_chunks // PIPE_BODY
    lax.fori_loop(0, n_body, single_pass_body, 0)
    for j in range(PIPE_BODY * n_body, n_chunks):
        single_pass_block(j)
    finish()

    @pl.when(jnp.max(grow_ref[...]) > MAX_GROWTH_LOG2)
    def _():
        m_ref[...] = jnp.full_like(m_ref, -jnp.inf)
        acc_ref[...] = jnp.zeros_like(acc_ref)

        def stage(slot, k):
            st = jnp.dot(k, qs_ref[...], preferred_element_type=F32)
            s_ref[slot, 0:k.shape[0], :] = st
            mx_ref[slot] = jnp.max(st, axis=0, keepdims=True)

        def consume(slot, n, vt):
            m_prev = m_ref[...]
            m_new = jnp.maximum(m_prev, mx_ref[slot])
            alpha = jnp.exp2(m_prev - m_new)
            pt = jnp.exp2(s_ref[slot, 0:n, :] - m_new).astype(BF16)
            acc_ref[...] = alpha * acc_ref[...] + jnp.dot(vt, pt, preferred_element_type=F32)
            m_ref[...] = m_new

        stage(1, kc_ref[...])
        stage(0, lat_k(0))
        consume(1, n_ctx, vtc_ref[0])

        def pair(jj, carry):
            j = 2 * jj + 1
            stage(1, lat_k(j))
            consume(0, tk, vtl_ref[j - 1])
            stage(0, lat_k(j + 1))
            consume(1, tk, vtl_ref[j])
            return carry

        n_pairs = (n_chunks - 1) // 2
        lax.fori_loop(0, n_pairs, pair, 0)
        slot = 1
        for j in range(2 * n_pairs + 1, n_chunks):
            stage(slot, lat_k(j))
            consume(1 - slot, tk, vtl_ref[j - 1])
            slot = 1 - slot
        consume(1 - slot, tk, vtl_ref[n_chunks - 1])
        finish()


def _attn(qt, k_ctx, vt_ctx, k_lat, vt_lat, lam_params, subln_g, lambda_init, tq):
    b, heads, _, n_lat = qt.shape
    n_ctx = k_ctx.shape[2]
    n_chunks, _, tk = vt_lat.shape[2:]
    assert n_ctx <= tk
    kern = functools.partial(_attn_kernel, tq=tq, lambda_init=lambda_init)
    vt_block = lambda a: pl.BlockSpec((None, None) + a.shape[2:], lambda b, h, i: (b, h, 0, 0, 0))
    return pl.pallas_call(
        kern,
        grid=(b, heads, n_lat // tq),
        in_specs=[pl.BlockSpec((None, None, HEAD_W, tq), lambda b, h, i: (b, h, 0, i)),
                  pl.BlockSpec((None, None, n_ctx, HEAD_W), lambda b, h, i: (b, h, 0, 0)),
                  vt_block(vt_ctx),
                  pl.BlockSpec((None, None, n_lat, HEAD_W), lambda b, h, i: (b, h, 0, 0)),
                  vt_block(vt_lat),
                  pl.BlockSpec(lam_params.shape, lambda b, h, i: (0, 0)),
                  pl.BlockSpec(subln_g.shape, lambda b, h, i: (0, 0))],
        out_specs=pl.BlockSpec((None, tq, HEAD_W), lambda b, h, i: (b, i, h)),
        out_shape=jax.ShapeDtypeStruct((b, n_lat, heads * HEAD_W), BF16),
        scratch_shapes=[pltpu.VMEM((HEAD_W, 2 * tq), BF16), pltpu.VMEM((2, tk, 2 * tq), F32),
                        pltpu.VMEM((2, 1, 2 * tq), F32), pltpu.VMEM((1, 2 * tq), F32),
                        pltpu.VMEM((VT_ROWS, 2 * tq), F32), pltpu.VMEM((1, 2 * tq), F32)],
        compiler_params=_params(3),
        name="attn",
    )(qt, k_ctx, vt_ctx, k_lat, vt_lat, lam_params, subln_g)


def _proj_kernel(o_ref, h_ref, mod_ref, w_ref, out_ref):
    out_ref[...] = h_ref[...] + mod_ref[2:3, :] * jnp.dot(o_ref[...], w_ref[...], preferred_element_type=F32)


def _proj(o, h, mod, w, tm):
    b, l, d = h.shape
    return pl.pallas_call(
        _proj_kernel,
        grid=(b, l // tm),
        in_specs=[pl.BlockSpec((None, tm, o.shape[2]), lambda b, i: (b, i, 0)),
                  pl.BlockSpec((None, tm, d), lambda b, i: (b, i, 0)),
                  _mod_spec(d, None), _resident(w.shape)],
        out_specs=pl.BlockSpec((None, tm, d), lambda b, i: (b, i, 0)),
        out_shape=jax.ShapeDtypeStruct(h.shape, F32),
        compiler_params=_params(2),
        name="proj",
    )(o, h, mod, w)


def _tile(l, target):
    return min(l, target)


def kernel(x, c, ctx, c_ctx, ada_w, ada_b, gm_w_in, gm_norm_g, gm_w_s, gm_b_s, gm_w_out, da_w_qkv, da_lambda_q1, da_lambda_k1, da_lambda_q2, da_lambda_k2, da_subln_g, da_w_out, ffn_w_up, ffn_conv_w, ffn_conv_b, ffn_w_down, final_norm_g):
    bsz, n_lat, d = x.shape
    depth = ada_w.shape[0]
    ffn_dim = ffn_w_down.shape[1]
    assert depth == 2 and bsz < ADA_ROWS and n_lat % GRID_W == 0
    ctx_row = bsz
    ffn_cw = 2 * LANES
    n_chunks = ffn_dim // ffn_cw
    assert n_chunks * ffn_cw == ffn_dim

    cond = jnp.zeros((ADA_ROWS, d), F32).at[:bsz].set(c).at[ctx_row].set(c_ctx)
    mod = _ada(cond, ada_w, ada_b).reshape(depth, ADA_ROWS, 6, d)

    def ffn_weights(i):
        w_up = ffn_w_up[i].astype(BF16).reshape(d, 2, n_chunks, ffn_cw).transpose(1, 2, 0, 3)
        cw = ffn_conv_w[i].reshape(CONV_W, 2, n_chunks, ffn_cw).transpose(1, 2, 0, 3)
        cb = ffn_conv_b[i].reshape(2, n_chunks, 1, ffn_cw)
        w_dn = ffn_w_down[i].astype(BF16).reshape(n_chunks, ffn_cw, d)
        return w_up, cw, cb, w_dn

    tm_lat = _tile(n_lat, 512)
    tm_ctx = _tile(ctx.shape[1], 512)

    gm = (gm_w_in[0].astype(BF16), gm_norm_g[0].reshape(1, -1), gm_w_s[0].astype(BF16),
          gm_b_s[0][:, :, None], gm_w_out[0].astype(BF16))
    fw = ffn_weights(0)
    h = _gmlp(x, mod[0], None, *gm, tm_lat)
    hc = _gmlp(ctx, mod[0], ctx_row, *gm, tm_ctx)
    h = _ffn(h, mod[0], None, *fw, None, tm_lat)
    hc = _ffn(hc, mod[0], ctx_row, *fw, None, tm_ctx)

    lambda_init = 0.8 - 0.6 * math.exp(-0.3 * 1)
    w_qkv = da_w_qkv[0].astype(BF16)
    da_width = w_qkv.shape[1] // 3
    qt, k_lat, vt_lat = _qkv(h, mod[1], None, w_qkv, _rope_tables(n_lat), tm_lat)
    _, k_ctx, vt_ctx = _qkv(hc, mod[1], ctx_row, w_qkv[:, da_width:], None, tm_ctx)
    lam_params = jnp.stack([da_lambda_q1[0], da_lambda_k1[0], da_lambda_q2[0], da_lambda_k2[0]])
    o = _attn(qt, k_ctx, vt_ctx, k_lat, vt_lat, lam_params, da_subln_g[0].reshape(-1, 1), lambda_init, 512)
    h = _proj(o, h, mod[1], da_w_out[0].astype(BF16), tm_lat)
    fw = ffn_weights(1)
    return _ffn(h, mod[1], None, *fw, final_norm_g.reshape(1, -1), tm_lat)
```

```python
import functools
import math

import jax
import jax.numpy as jnp
from jax import lax
from jax.experimental import pallas as pl
from jax.experimental.pallas import tpu as pltpu

EPS = 1e-6
GRID_W = 64
GM_GROUPS = 8
CHUNK = 128
SUB_HEAD = 64
HEAD_W = 2 * SUB_HEAD
VT_ROWS = HEAD_W + 16
ROPE_THETA = 10000.0
CONV_W = 3
HALO = 16
LANES = 128
ADA_ROWS = 16
PIPE_BODY = 16
MAX_GROWTH_LOG2 = 32.0
FFN_PIPE_BODY = 2
VMEM_LIMIT_BYTES = 56 * 1024 * 1024

F32 = jnp.float32
BF16 = jnp.bfloat16


def _rms(x):
    return x * lax.rsqrt(jnp.mean(x * x, axis=-1, keepdims=True) + EPS)


def _modulate(x, mod_ref, k):
    return _rms(x) * (1.0 + mod_ref[k + 1:k + 2, :]) + mod_ref[k:k + 1, :]


def _silu(x):
    return x * (1.0 / (1.0 + jnp.exp(-x)))


def _gelu_tanh(x):
    return 0.5 * x * (1.0 + jnp.tanh(math.sqrt(2.0 / math.pi) * (x + 0.044715 * (x * x * x))))


def _params(n_axes):
    return pltpu.CompilerParams(dimension_semantics=("arbitrary",) * n_axes,
                                vmem_limit_bytes=VMEM_LIMIT_BYTES)


def _resident(shape):
    nd = len(shape)
    return pl.BlockSpec(shape, lambda *_: (0,) * nd, pipeline_mode=pl.Buffered(1))


def _ada_kernel(cond_ref, w_ref, b_ref, out_ref):
    s = _silu(cond_ref[...])
    out_ref[...] = jnp.dot(s, w_ref[...], preferred_element_type=F32,
                           precision=lax.Precision.HIGHEST) + b_ref[...]


def _ada(cond, ada_w, ada_b):
    depth, d, n = ada_w.shape
    nb = 6 * LANES * 2
    return pl.pallas_call(
        _ada_kernel,
        grid=(depth, n // nb),
        in_specs=[pl.BlockSpec((ADA_ROWS, d), lambda l, j: (0, 0)),
                  pl.BlockSpec((None, d, nb), lambda l, j: (l, 0, j)),
                  pl.BlockSpec((None, 1, nb), lambda l, j: (l, 0, j))],
        out_specs=pl.BlockSpec((None, ADA_ROWS, nb), lambda l, j: (l, 0, j)),
        out_shape=jax.ShapeDtypeStruct((depth, ADA_ROWS, n), F32),
        compiler_params=_params(2),
        name="ada",
    )(cond, ada_w, ada_b.reshape(depth, 1, n))


def _mod_spec(d, ctx_row):
    if ctx_row is None:
        return pl.BlockSpec((None, 6, d), lambda b, i: (b, 0, 0))
    return pl.BlockSpec((None, 6, d), lambda b, i: (ctx_row, 0, 0))


def _gmlp_kernel(h_ref, mod_ref, win_ref, ng_ref, ws_ref, bs_ref, wout_ref, out_ref, vn_ref, uv_ref, *, tm, gw):
    x = h_ref[...]
    xl = _modulate(x, mod_ref, 0).astype(BF16)
    v = _gelu_tanh(jnp.dot(xl, win_ref[:, gw:], preferred_element_type=F32))
    vn_ref[...] = (_rms(v) * ng_ref[...]).astype(BF16)
    gd = gw // GM_GROUPS
    for g in range(GM_GROUPS):
        cols = slice(g * gd, (g + 1) * gd)
        u = _gelu_tanh(jnp.dot(xl, win_ref[:, cols], preferred_element_type=F32))
        for c in range(tm // CHUNK):
            rows = slice(c * CHUNK, (c + 1) * CHUNK)
            mix = jnp.dot(ws_ref[g], vn_ref[rows, cols], preferred_element_type=F32) + bs_ref[g]
            uv_ref[rows, cols] = (u[rows, :] * mix).astype(BF16)
    out = jnp.dot(uv_ref[...], wout_ref[...], preferred_element_type=F32)
    out_ref[...] = x + mod_ref[2:3, :] * out


def _gmlp(h, mod, ctx_row, w_in, norm_g, w_s, b_s, w_out, tm):
    b, l, d = h.shape
    gw = w_out.shape[0]
    kern = functools.partial(_gmlp_kernel, tm=tm, gw=gw)
    return pl.pallas_call(
        kern,
        grid=(b, l // tm),
        in_specs=[pl.BlockSpec((None, tm, d), lambda b, i: (b, i, 0)),
                  _mod_spec(d, ctx_row),
                  _resident(w_in.shape), _resident(norm_g.shape), _resident(w_s.shape),
                  _resident(b_s.shape), _resident(w_out.shape)],
        out_specs=pl.BlockSpec((None, tm, d), lambda b, i: (b, i, 0)),
        out_shape=jax.ShapeDtypeStruct(h.shape, F32),
        scratch_shapes=[pltpu.VMEM((tm, gw), BF16), pltpu.VMEM((tm, gw), BF16)],
        compiler_params=_params(2),
        name="gmlp",
    )(h, mod, w_in, norm_g, w_s, b_s, w_out)


def _ffn_kernel(*refs, tm, n_chunks, final):
    if final:
        h_ref, hp_ref, hn_ref, mod_ref, wup_ref, cw_ref, cb_ref, wdn_ref, fg_ref, out_ref, xm_ref, acc_ref, z_ref = refs
    else:
        h_ref, hp_ref, hn_ref, mod_ref, wup_ref, cw_ref, cb_ref, wdn_ref, out_ref, xm_ref, acc_ref, z_ref = refs
    i = pl.program_id(1)
    last = pl.num_programs(1) - 1
    x = h_ref[...]
    xm_ref[0:HALO, :] = jnp.where(i == 0, 0.0, _modulate(hp_ref[...], mod_ref, 3)).astype(BF16)
    xm_ref[HALO:HALO + tm, :] = _modulate(x, mod_ref, 3).astype(BF16)
    xm_ref[HALO + tm:, :] = jnp.where(i == last, 0.0, _modulate(hn_ref[...], mod_ref, 3)).astype(BF16)
    acc_ref[...] = jnp.zeros_like(acc_ref)

    def stage(slot, c):
        xm = xm_ref[...]
        for s in range(2):
            z_ref[slot, s] = jnp.dot(xm, wup_ref[s, c], preferred_element_type=F32)

    def consume(slot, c):
        def conv_half(s):
            cw = cw_ref[s, c]
            taps = [z_ref[slot, s, HALO - 1 + j:HALO - 1 + j + tm, :] * cw[j:j + 1, :] for j in range(CONV_W)]
            return taps[0] + taps[1] + taps[2] + cb_ref[s, c]

        act = (_silu(conv_half(0)) * conv_half(1)).astype(BF16)
        acc_ref[...] += jnp.dot(act, wdn_ref[c], preferred_element_type=F32)

    def body(jj, carry):
        c = FFN_PIPE_BODY * jj
        for t in range(FFN_PIPE_BODY):
            stage((t + 1) % 2, c + t + 1)
            consume(t % 2, c + t)
        return carry

    n_body = (n_chunks - 1) // FFN_PIPE_BODY
    stage(0, 0)
    lax.fori_loop(0, n_body, body, 0)
    slot = 0
    for c in range(FFN_PIPE_BODY * n_body, n_chunks - 1):
        stage(1 - slot, c + 1)
        consume(slot, c)
        slot = 1 - slot
    consume(slot, n_chunks - 1)
    y = x + mod_ref[5:6, :] * acc_ref[...]
    if final:
        y = _rms(y) * fg_ref[...]
    out_ref[...] = y


def _ffn(h, mod, ctx_row, w_up, conv_w, conv_b, w_down, final_g, tm):
    b, l, d = h.shape
    n_chunks = w_down.shape[0]
    nh = l // HALO
    per = tm // HALO
    final = final_g is not None
    kern = functools.partial(_ffn_kernel, tm=tm, n_chunks=n_chunks, final=final)
    in_specs = [pl.BlockSpec((None, tm, d), lambda b, i: (b, i, 0)),
                pl.BlockSpec((None, HALO, d), lambda b, i: (b, jnp.maximum(i * per - 1, 0), 0)),
                pl.BlockSpec((None, HALO, d), lambda b, i: (b, jnp.minimum((i + 1) * per, nh - 1), 0)),
                _mod_spec(d, ctx_row),
                _resident(w_up.shape), _resident(conv_w.shape), _resident(conv_b.shape), _resident(w_down.shape)]
    args = [h, h, h, mod, w_up, conv_w, conv_b, w_down]
    if final:
        in_specs.append(_resident(final_g.shape))
        args.append(final_g)
    return pl.pallas_call(
        kern,
        grid=(b, l // tm),
        in_specs=in_specs,
        out_specs=pl.BlockSpec((None, tm, d), lambda b, i: (b, i, 0)),
        out_shape=jax.ShapeDtypeStruct(h.shape, F32),
        scratch_shapes=[pltpu.VMEM((tm + 2 * HALO, d), BF16), pltpu.VMEM((tm, d), F32),
                        pltpu.VMEM((2, 2, tm + 2 * HALO, w_up.shape[3]), F32)],
        compiler_params=_params(2),
        name="ffn",
    )(*args)


def _qkv_kernel(*refs, heads, with_q):
    if with_q:
        h_ref, mod_ref, w_ref, cos_ref, sa_ref, sb_ref, qt_ref, k_ref, vt_ref = refs
    else:
        h_ref, mod_ref, w_ref, k_ref, vt_ref = refs
    xl = _modulate(h_ref[...], mod_ref, 0).astype(BF16)

    def rope(yb):
        return (yb * cos_ref[...] + pltpu.roll(yb, LANES - SUB_HEAD // 4, 1) * sa_ref[...]
                + pltpu.roll(yb, SUB_HEAD // 4, 1) * sb_ref[...])

    group = 4
    for part in range(3 if with_q else 2):
        kind = part if with_q else part + 1
        for h0 in range(0, heads, group):
            c0 = (part * heads + h0) * HEAD_W
            y = jnp.dot(xl, w_ref[:, c0:c0 + group * HEAD_W], preferred_element_type=F32)
            for hh in range(group):
                h = h0 + hh
                yb = y[:, hh * HEAD_W:(hh + 1) * HEAD_W]
                if kind == 0:
                    qt_ref[h] = (rope(yb) * (SUB_HEAD ** -0.5 * math.log2(math.e))).T.astype(BF16)
                elif kind == 1:
                    k_ref[h] = (rope(yb) if with_q else yb).astype(BF16)
                else:
                    vt_ref[h, 0:HEAD_W, :] = yb.T.astype(BF16)
                    vt_ref[h, HEAD_W:VT_ROWS, :] = jnp.ones((VT_ROWS - HEAD_W, yb.shape[0]), BF16)


def _qkv(h, mod, ctx_row, w, tables, tm):
    b, l, d = h.shape
    with_q = tables is not None
    heads = w.shape[1] // HEAD_W // (3 if with_q else 2)
    kern = functools.partial(_qkv_kernel, heads=heads, with_q=with_q)
    in_specs = [pl.BlockSpec((None, tm, d), lambda b, i: (b, i, 0)), _mod_spec(d, ctx_row), _resident(w.shape)]
    args = [h, mod, w]
    out_specs = [pl.BlockSpec((None, heads, tm, HEAD_W), lambda b, i: (b, 0, i, 0)),
                 pl.BlockSpec((None, heads, None, VT_ROWS, tm), lambda b, i: (b, 0, i, 0, 0))]
    out_shape = [jax.ShapeDtypeStruct((b, heads, l, HEAD_W), BF16),
                 jax.ShapeDtypeStruct((b, heads, l // tm, VT_ROWS, tm), BF16)]
    if with_q:
        in_specs += [pl.BlockSpec((tm, LANES), lambda b, i: (i, 0))] * 3
        args += list(tables)
        out_specs.insert(0, pl.BlockSpec((None, heads, HEAD_W, tm), lambda b, i: (b, 0, 0, i)))
        out_shape.insert(0, jax.ShapeDtypeStruct((b, heads, HEAD_W, l), BF16))
    outs = pl.pallas_call(
        kern,
        grid=(b, l // tm),
        in_specs=in_specs,
        out_specs=out_specs,
        out_shape=out_shape,
        compiler_params=_params(2),
        name="qkv",
    )(*args)
    return outs if with_q else (None, *outs)


def _rope_tables(n_lat):
    rows = n_lat // GRID_W
    row_pos = jnp.broadcast_to(jnp.arange(rows, dtype=F32)[:, None], (rows, GRID_W)).reshape(-1)
    col_pos = jnp.broadcast_to(jnp.arange(GRID_W, dtype=F32)[None, :], (rows, GRID_W)).reshape(-1)
    n_freq = SUB_HEAD // 4
    inv_freq = ROPE_THETA ** (-jnp.arange(n_freq, dtype=F32) / n_freq)
    ang_r = row_pos[:, None] * inv_freq
    ang_c = col_pos[:, None] * inv_freq
    ang = jnp.concatenate([ang_r, ang_r, ang_c, ang_c] * 2, axis=-1)
    first = (jnp.arange(LANES) % (2 * n_freq)) < n_freq
    sin = jnp.sin(ang)
    return jnp.cos(ang), jnp.where(first, -sin, 0.0), jnp.where(first, 0.0, sin)


def _attn_kernel(qt_ref, kc_ref, vtc_ref, kl_ref, vtl_ref, lam_ref, g_ref, o_ref,
                 qs_ref, s_ref, mx_ref, m_ref, acc_ref, grow_ref, *, tq, lambda_init):
    dim = lax.broadcasted_iota(jnp.int32, (HEAD_W, tq), 0)
    qt = qt_ref[...]
    zero = jnp.zeros_like(qt)
    qs_ref[:, 0:tq] = jnp.where(dim < SUB_HEAD, qt, zero)
    qs_ref[:, tq:2 * tq] = jnp.where(dim >= SUB_HEAD, qt, zero)
    n_chunks, _, tk = vtl_ref.shape
    n_ctx = kc_ref.shape[0]

    def lat_k(j):
        return kl_ref[pl.ds(pl.multiple_of(j * tk, tk), tk), :]

    def finish():
        lp = lam_ref[...]
        lam = (jnp.exp(jnp.sum(lp[0:1, :] * lp[1:2, :], axis=1, keepdims=True))
               - jnp.exp(jnp.sum(lp[2:3, :] * lp[3:4, :], axis=1, keepdims=True)) + lambda_init)
        ot = acc_ref[0:HEAD_W, :] / acc_ref[HEAD_W:HEAD_W + 1, :]
        od = ot[:, 0:tq] - lam * ot[:, tq:2 * tq]
        od = od * lax.rsqrt(jnp.mean(od * od, axis=0, keepdims=True) + EPS) * (g_ref[...] * (1.0 - lambda_init))
        o_ref[...] = od.T.astype(BF16)

    def single_pass_block(j):
        st = jnp.dot(lat_k(j), qs_ref[...], preferred_element_type=F32)
        m_prev = m_ref[...]
        pt = jnp.exp2(st - m_prev).astype(BF16)
        mx = jnp.max(st, axis=0, keepdims=True)
        m_new = jnp.maximum(m_prev, mx)
        grow_ref[...] = jnp.maximum(grow_ref[...], mx - m_prev)
        pv = jnp.dot(vtl_ref[j], pt, preferred_element_type=F32)
        acc_ref[...] = (acc_ref[...] + pv) * jnp.exp2(m_prev - m_new)
        m_ref[...] = m_new

    st = jnp.dot(kc_ref[...], qs_ref[...], preferred_element_type=F32)
    m_ref[...] = jnp.max(st, axis=0, keepdims=True)
    acc_ref[...] = jnp.dot(vtc_ref[0], jnp.exp2(st - m_ref[...]).astype(BF16), preferred_element_type=F32)
    grow_ref[...] = jnp.zeros_like(grow_ref)

    def single_pass_body(jj, carry):
        for t in range(PIPE_BODY):
            single_pass_block(PIPE_BODY * jj + t)
        return carry

    n_body = n_chunks // PIPE_BODY
    lax.fori_loop(0, n_body, single_pass_body, 0)
    for j in range(PIPE_BODY * n_body, n_chunks):
        single_pass_block(j)
    finish()

    @pl.when(jnp.max(grow_ref[...]) > MAX_GROWTH_LOG2)
    def _():
        m_ref[...] = jnp.full_like(m_ref, -jnp.inf)
        acc_ref[...] = jnp.zeros_like(acc_ref)

        def stage(slot, k):
            st = jnp.dot(k, qs_ref[...], preferred_element_type=F32)
            s_ref[slot, 0:k.shape[0], :] = st
            mx_ref[slot] = jnp.max(st, axis=0, keepdims=True)

        def consume(slot, n, vt):
            m_prev = m_ref[...]
            m_new = jnp.maximum(m_prev, mx_ref[slot])
            alpha = jnp.exp2(m_prev - m_new)
            pt = jnp.exp2(s_ref[slot, 0:n, :] - m_new).astype(BF16)
            acc_ref[...] = alpha * acc_ref[...] + jnp.dot(vt, pt, preferred_element_type=F32)
            m_ref[...] = m_new

        stage(1, kc_ref[...])
        stage(0, lat_k(0))
        consume(1, n_ctx, vtc_ref[0])

        def pair(jj, carry):
            j = 2 * jj + 1
            stage(1, lat_k(j))
            consume(0, tk, vtl_ref[j - 1])
            stage(0, lat_k(j + 1))
            consume(1, tk, vtl_ref[j])
            return carry

        n_pairs = (n_chunks - 1) // 2
        lax.fori_loop(0, n_pairs, pair, 0)
        slot = 1
        for j in range(2 * n_pairs + 1, n_chunks):
            stage(slot, lat_k(j))
            consume(1 - slot, tk, vtl_ref[j - 1])
            slot = 1 - slot
        consume(1 - slot, tk, vtl_ref[n_chunks - 1])
        finish()


def _attn(qt, k_ctx, vt_ctx, k_lat, vt_lat, lam_params, subln_g, lambda_init, tq):
    b, heads, _, n_lat = qt.shape
    n_ctx = k_ctx.shape[2]
    n_chunks, _, tk = vt_lat.shape[2:]
    assert n_ctx <= tk
    kern = functools.partial(_attn_kernel, tq=tq, lambda_init=lambda_init)
    vt_block = lambda a: pl.BlockSpec((None, None) + a.shape[2:], lambda b, h, i: (b, h, 0, 0, 0))
    return pl.pallas_call(
        kern,
        grid=(b, heads, n_lat // tq),
        in_specs=[pl.BlockSpec((None, None, HEAD_W, tq), lambda b, h, i: (b, h, 0, i)),
                  pl.BlockSpec((None, None, n_ctx, HEAD_W), lambda b, h, i: (b, h, 0, 0)),
                  vt_block(vt_ctx),
                  pl.BlockSpec((None, None, n_lat, HEAD_W), lambda b, h, i: (b, h, 0, 0)),
                  vt_block(vt_lat),
                  pl.BlockSpec(lam_params.shape, lambda b, h, i: (0, 0)),
                  pl.BlockSpec(subln_g.shape, lambda b, h, i: (0, 0))],
        out_specs=pl.BlockSpec((None, tq, HEAD_W), lambda b, h, i: (b, i, h)),
        out_shape=jax.ShapeDtypeStruct((b, n_lat, heads * HEAD_W), BF16),
        scratch_shapes=[pltpu.VMEM((HEAD_W, 2 * tq), BF16), pltpu.VMEM((2, tk, 2 * tq), F32),
                        pltpu.VMEM((2, 1, 2 * tq), F32), pltpu.VMEM((1, 2 * tq), F32),
                        pltpu.VMEM((VT_ROWS, 2 * tq), F32), pltpu.VMEM((1, 2 * tq), F32)],
        compiler_params=_params(3),
        name="attn",
    )(qt, k_ctx, vt_ctx, k_lat, vt_lat, lam_params, subln_g)


def _proj_kernel(o_ref, h_ref, mod_ref, w_ref, out_ref):
    out_ref[...] = h_ref[...] + mod_ref[2:3, :] * jnp.dot(o_ref[...], w_ref[...], preferred_element_type=F32)


def _proj(o, h, mod, w, tm):
    b, l, d = h.shape
    return pl.pallas_call(
        _proj_kernel,
        grid=(b, l // tm),
        in_specs=[pl.BlockSpec((None, tm, o.shape[2]), lambda b, i: (b, i, 0)),
                  pl.BlockSpec((None, tm, d), lambda b, i: (b, i, 0)),
                  _mod_spec(d, None), _resident(w.shape)],
        out_specs=pl.BlockSpec((None, tm, d), lambda b, i: (b, i, 0)),
        out_shape=jax.ShapeDtypeStruct(h.shape, F32),
        compiler_params=_params(2),
        name="proj",
    )(o, h, mod, w)


def _tile(l, target):
    return min(l, target)


def kernel(x, c, ctx, c_ctx, ada_w, ada_b, gm_w_in, gm_norm_g, gm_w_s, gm_b_s, gm_w_out, da_w_qkv, da_lambda_q1, da_lambda_k1, da_lambda_q2, da_lambda_k2, da_subln_g, da_w_out, ffn_w_up, ffn_conv_w, ffn_conv_b, ffn_w_down, final_norm_g):
    bsz, n_lat, d = x.shape
    depth = ada_w.shape[0]
    ffn_dim = ffn_w_down.shape[1]
    assert depth == 2 and bsz < ADA_ROWS and n_lat % GRID_W == 0
    ctx_row = bsz
    ffn_cw = 2 * LANES
    n_chunks = ffn_dim // ffn_cw
    assert n_chunks * ffn_cw == ffn_dim

    cond = jnp.zeros((ADA_ROWS, d), F32).at[:bsz].set(c).at[ctx_row].set(c_ctx)
    mod = _ada(cond, ada_w, ada_b).reshape(depth, ADA_ROWS, 6, d)

    def ffn_weights(i):
        w_up = ffn_w_up[i].astype(BF16).reshape(d, 2, n_chunks, ffn_cw).transpose(1, 2, 0, 3)
        cw = ffn_conv_w[i].reshape(CONV_W, 2, n_chunks, ffn_cw).transpose(1, 2, 0, 3)
        cb = ffn_conv_b[i].reshape(2, n_chunks, 1, ffn_cw)
        w_dn = ffn_w_down[i].astype(BF16).reshape(n_chunks, ffn_cw, d)
        return w_up, cw, cb, w_dn

    tm_lat = _tile(n_lat, 512)
    tm_ctx = _tile(ctx.shape[1], 512)

    gm = (gm_w_in[0].astype(BF16), gm_norm_g[0].reshape(1, -1), gm_w_s[0].astype(BF16),
          gm_b_s[0][:, :, None], gm_w_out[0].astype(BF16))
    fw = ffn_weights(0)
    h = _gmlp(x, mod[0], None, *gm, tm_lat)
    hc = _gmlp(ctx, mod[0], ctx_row, *gm, tm_ctx)
    h = _ffn(h, mod[0], None, *fw, None, _tile(n_lat, 1024))
    hc = _ffn(hc, mod[0], ctx_row, *fw, None, tm_ctx)

    lambda_init = 0.8 - 0.6 * math.exp(-0.3 * 1)
    w_qkv = da_w_qkv[0].astype(BF16)
    da_width = w_qkv.shape[1] // 3
    qt, k_lat, vt_lat = _qkv(h, mod[1], None, w_qkv, _rope_tables(n_lat), tm_lat)
    _, k_ctx, vt_ctx = _qkv(hc, mod[1], ctx_row, w_qkv[:, da_width:], None, tm_ctx)
    lam_params = jnp.stack([da_lambda_q1[0], da_lambda_k1[0], da_lambda_q2[0], da_lambda_k2[0]])
    o = _attn(qt, k_ctx, vt_ctx, k_lat, vt_lat, lam_params, da_subln_g[0].reshape(-1, 1), lambda_init, 1024)
    h = _proj(o, h, mod[1], da_w_out[0].astype(BF16), tm_lat)
    fw = ffn_weights(1)
    return _ffn(h, mod[1], None, *fw, final_norm_g.reshape(1, -1), _tile(n_lat, 1024))
```

```python
import functools
import math

import jax
import jax.numpy as jnp
from jax import lax
from jax.experimental import pallas as pl
from jax.experimental.pallas import tpu as pltpu

EPS = 1e-6
GRID_W = 64
GM_GROUPS = 8
CHUNK = 128
SUB_HEAD = 64
HEAD_W = 2 * SUB_HEAD
VT_ROWS = HEAD_W + 16
ROPE_THETA = 10000.0
CONV_W = 3
HALO = 16
LANES = 128
ADA_ROWS = 16
PIPE_BODY = 16
MAX_GROWTH_LOG2 = 32.0
FFN_PIPE_BODY = 2
VMEM_LIMIT_BYTES = 60 * 1024 * 1024

F32 = jnp.float32
BF16 = jnp.bfloat16


def _rms(x):
    return x * lax.rsqrt(jnp.mean(x * x, axis=-1, keepdims=True) + EPS)


def _modulate(x, mod_ref, k):
    return _rms(x) * (1.0 + mod_ref[k + 1:k + 2, :]) + mod_ref[k:k + 1, :]


def _silu(x):
    return x * (1.0 / (1.0 + jnp.exp(-x)))


def _gelu_tanh(x):
    return 0.5 * x * (1.0 + jnp.tanh(math.sqrt(2.0 / math.pi) * (x + 0.044715 * (x * x * x))))


def _params(n_axes):
    return pltpu.CompilerParams(dimension_semantics=("arbitrary",) * n_axes,
                                vmem_limit_bytes=VMEM_LIMIT_BYTES)


def _resident(shape):
    nd = len(shape)
    return pl.BlockSpec(shape, lambda *_: (0,) * nd, pipeline_mode=pl.Buffered(1))


def _ada_kernel(cond_ref, w_ref, b_ref, out_ref):
    s = _silu(cond_ref[...])
    out_ref[...] = jnp.dot(s, w_ref[...], preferred_element_type=F32,
                           precision=lax.Precision.HIGHEST) + b_ref[...]


def _ada(cond, ada_w, ada_b):
    depth, d, n = ada_w.shape
    nb = 6 * LANES * 2
    return pl.pallas_call(
        _ada_kernel,
        grid=(depth, n // nb),
        in_specs=[pl.BlockSpec((ADA_ROWS, d), lambda l, j: (0, 0)),
                  pl.BlockSpec((None, d, nb), lambda l, j: (l, 0, j)),
                  pl.BlockSpec((None, 1, nb), lambda l, j: (l, 0, j))],
        out_specs=pl.BlockSpec((None, ADA_ROWS, nb), lambda l, j: (l, 0, j)),
        out_shape=jax.ShapeDtypeStruct((depth, ADA_ROWS, n), F32),
        compiler_params=_params(2),
        name="ada",
    )(cond, ada_w, ada_b.reshape(depth, 1, n))


def _mod_spec(d, ctx_row):
    if ctx_row is None:
        return pl.BlockSpec((None, 6, d), lambda b, i: (b, 0, 0))
    return pl.BlockSpec((None, 6, d), lambda b, i: (ctx_row, 0, 0))


def _gmlp_kernel(h_ref, mod_ref, win_ref, ng_ref, ws_ref, bs_ref, wout_ref, out_ref, vn_ref, uv_ref, *, tm, gw):
    x = h_ref[...]
    xl = _modulate(x, mod_ref, 0).astype(BF16)
    v = _gelu_tanh(jnp.dot(xl, win_ref[:, gw:], preferred_element_type=F32))
    vn_ref[...] = (_rms(v) * ng_ref[...]).astype(BF16)
    gd = gw // GM_GROUPS
    for g in range(GM_GROUPS):
        cols = slice(g * gd, (g + 1) * gd)
        u = _gelu_tanh(jnp.dot(xl, win_ref[:, cols], preferred_element_type=F32))
        for c in range(tm // CHUNK):
            rows = slice(c * CHUNK, (c + 1) * CHUNK)
            mix = jnp.dot(ws_ref[g], vn_ref[rows, cols], preferred_element_type=F32) + bs_ref[g]
            uv_ref[rows, cols] = (u[rows, :] * mix).astype(BF16)
    out = jnp.dot(uv_ref[...], wout_ref[...], preferred_element_type=F32)
    out_ref[...] = x + mod_ref[2:3, :] * out


def _gmlp(h, mod, ctx_row, w_in, norm_g, w_s, b_s, w_out, tm):
    b, l, d = h.shape
    gw = w_out.shape[0]
    kern = functools.partial(_gmlp_kernel, tm=tm, gw=gw)
    return pl.pallas_call(
        kern,
        grid=(b, l // tm),
        in_specs=[pl.BlockSpec((None, tm, d), lambda b, i: (b, i, 0)),
                  _mod_spec(d, ctx_row),
                  _resident(w_in.shape), _resident(norm_g.shape), _resident(w_s.shape),
                  _resident(b_s.shape), _resident(w_out.shape)],
        out_specs=pl.BlockSpec((None, tm, d), lambda b, i: (b, i, 0)),
        out_shape=jax.ShapeDtypeStruct(h.shape, F32),
        scratch_shapes=[pltpu.VMEM((tm, gw), BF16), pltpu.VMEM((tm, gw), BF16)],
        compiler_params=_params(2),
        name="gmlp",
    )(h, mod, w_in, norm_g, w_s, b_s, w_out)


def _ffn_kernel(*refs, tm, n_chunks, final, mixer):
    refs = list(refs)
    h_refs = refs[0:3]
    del refs[0:3]
    if mixer:
        o_refs, wproj_ref = refs[0:3], refs[3]
        del refs[0:4]
    mod_ref, wup_ref, cw_ref, cb_ref, wdn_ref = refs[0:5]
    del refs[0:5]
    if final:
        fg_ref = refs.pop(0)
    out_ref, xm_ref, acc_ref, z_ref = refs
    i = pl.program_id(1)
    last = pl.num_programs(1) - 1

    def stream(k):
        if not mixer:
            return h_refs[k][...]
        return h_refs[k][...] + mod_ref[2:3, :] * jnp.dot(o_refs[k][...], wproj_ref[...], preferred_element_type=F32)

    out_ref[...] = stream(0)
    xm_ref[0:HALO, :] = jnp.where(i == 0, 0.0, _modulate(stream(1), mod_ref, 3)).astype(BF16)
    xm_ref[HALO:HALO + tm, :] = _modulate(out_ref[...], mod_ref, 3).astype(BF16)
    xm_ref[HALO + tm:, :] = jnp.where(i == last, 0.0, _modulate(stream(2), mod_ref, 3)).astype(BF16)
    acc_ref[...] = jnp.zeros_like(acc_ref)

    def stage(slot, c):
        xm = xm_ref[...]
        for s in range(2):
            z_ref[slot, s] = jnp.dot(xm, wup_ref[s, c], preferred_element_type=F32)

    def consume(slot, c):
        def conv_half(s):
            cw = cw_ref[s, c]
            taps = [z_ref[slot, s, HALO - 1 + j:HALO - 1 + j + tm, :] * cw[j:j + 1, :] for j in range(CONV_W)]
            return taps[0] + taps[1] + taps[2] + cb_ref[s, c]

        act = (_silu(conv_half(0)) * conv_half(1)).astype(BF16)
        acc_ref[...] += jnp.dot(act, wdn_ref[c], preferred_element_type=F32)

    def body(jj, carry):
        c = FFN_PIPE_BODY * jj
        for t in range(FFN_PIPE_BODY):
            stage((t + 1) % 2, c + t + 1)
            consume(t % 2, c + t)
        return carry

    n_body = (n_chunks - 1) // FFN_PIPE_BODY
    stage(0, 0)
    lax.fori_loop(0, n_body, body, 0)
    slot = 0
    for c in range(FFN_PIPE_BODY * n_body, n_chunks - 1):
        stage(1 - slot, c + 1)
        consume(slot, c)
        slot = 1 - slot
    consume(slot, n_chunks - 1)
    y = out_ref[...] + mod_ref[5:6, :] * acc_ref[...]
    if final:
        y = _rms(y) * fg_ref[...]
    out_ref[...] = y


def _ffn(h, mod, ctx_row, w_up, conv_w, conv_b, w_down, final_g, tm, mixer=None):
    b, l, d = h.shape
    n_chunks = w_down.shape[0]
    nh = l // HALO
    per = tm // HALO
    final = final_g is not None
    kern = functools.partial(_ffn_kernel, tm=tm, n_chunks=n_chunks, final=final, mixer=mixer is not None)
    tile_specs = [pl.BlockSpec((None, tm, d), lambda b, i: (b, i, 0)),
                  pl.BlockSpec((None, HALO, d), lambda b, i: (b, jnp.maximum(i * per - 1, 0), 0)),
                  pl.BlockSpec((None, HALO, d), lambda b, i: (b, jnp.minimum((i + 1) * per, nh - 1), 0))]
    in_specs = list(tile_specs)
    args = [h, h, h]
    if mixer is not None:
        o, w_proj = mixer
        in_specs += tile_specs + [_resident(w_proj.shape)]
        args += [o, o, o, w_proj]
    in_specs += [_mod_spec(d, ctx_row),
                 _resident(w_up.shape), _resident(conv_w.shape), _resident(conv_b.shape), _resident(w_down.shape)]
    args += [mod, w_up, conv_w, conv_b, w_down]
    if final:
        in_specs.append(_resident(final_g.shape))
        args.append(final_g)
    return pl.pallas_call(
        kern,
        grid=(b, l // tm),
        in_specs=in_specs,
        out_specs=pl.BlockSpec((None, tm, d), lambda b, i: (b, i, 0)),
        out_shape=jax.ShapeDtypeStruct(h.shape, F32),
        scratch_shapes=[pltpu.VMEM((tm + 2 * HALO, d), BF16), pltpu.VMEM((tm, d), F32),
                        pltpu.VMEM((2, 2, tm + 2 * HALO, w_up.shape[3]), F32)],
        compiler_params=_params(2),
        name="ffn",
    )(*args)


def _qkv_kernel(*refs, heads, with_q):
    if with_q:
        h_ref, mod_ref, w_ref, cos_ref, sa_ref, sb_ref, qt_ref, k_ref, vt_ref = refs
    else:
        h_ref, mod_ref, w_ref, k_ref, vt_ref = refs
    xl = _modulate(h_ref[...], mod_ref, 0).astype(BF16)

    def rope(yb):
        return (yb * cos_ref[...] + pltpu.roll(yb, LANES - SUB_HEAD // 4, 1) * sa_ref[...]
                + pltpu.roll(yb, SUB_HEAD // 4, 1) * sb_ref[...])

    group = 4
    for part in range(3 if with_q else 2):
        kind = part if with_q else part + 1
        for h0 in range(0, heads, group):
            c0 = (part * heads + h0) * HEAD_W
            y = jnp.dot(xl, w_ref[:, c0:c0 + group * HEAD_W], preferred_element_type=F32)
            for hh in range(group):
                h = h0 + hh
                yb = y[:, hh * HEAD_W:(hh + 1) * HEAD_W]
                if kind == 0:
                    qt_ref[h] = (rope(yb) * (SUB_HEAD ** -0.5 * math.log2(math.e))).T.astype(BF16)
                elif kind == 1:
                    k_ref[h] = (rope(yb) if with_q else yb).astype(BF16)
                else:
                    vt_ref[h, 0:HEAD_W, :] = yb.T.astype(BF16)
                    vt_ref[h, HEAD_W:VT_ROWS, :] = jnp.ones((VT_ROWS - HEAD_W, yb.shape[0]), BF16)


def _qkv(h, mod, ctx_row, w, tables, tm):
    b, l, d = h.shape
    with_q = tables is not None
    heads = w.shape[1] // HEAD_W // (3 if with_q else 2)
    kern = functools.partial(_qkv_kernel, heads=heads, with_q=with_q)
    in_specs = [pl.BlockSpec((None, tm, d), lambda b, i: (b, i, 0)), _mod_spec(d, ctx_row), _resident(w.shape)]
    args = [h, mod, w]
    out_specs = [pl.BlockSpec((None, heads, tm, HEAD_W), lambda b, i: (b, 0, i, 0)),
                 pl.BlockSpec((None, heads, None, VT_ROWS, tm), lambda b, i: (b, 0, i, 0, 0))]
    out_shape = [jax.ShapeDtypeStruct((b, heads, l, HEAD_W), BF16),
                 jax.ShapeDtypeStruct((b, heads, l // tm, VT_ROWS, tm), BF16)]
    if with_q:
        in_specs += [pl.BlockSpec((tm, LANES), lambda b, i: (i, 0))] * 3
        args += list(tables)
        out_specs.insert(0, pl.BlockSpec((None, heads, HEAD_W, tm), lambda b, i: (b, 0, 0, i)))
        out_shape.insert(0, jax.ShapeDtypeStruct((b, heads, HEAD_W, l), BF16))
    outs = pl.pallas_call(
        kern,
        grid=(b, l // tm),
        in_specs=in_specs,
        out_specs=out_specs,
        out_shape=out_shape,
        compiler_params=_params(2),
        name="qkv",
    )(*args)
    return outs if with_q else (None, *outs)


def _rope_tables(n_lat):
    rows = n_lat // GRID_W
    row_pos = jnp.broadcast_to(jnp.arange(rows, dtype=F32)[:, None], (rows, GRID_W)).reshape(-1)
    col_pos = jnp.broadcast_to(jnp.arange(GRID_W, dtype=F32)[None, :], (rows, GRID_W)).reshape(-1)
    n_freq = SUB_HEAD // 4
    inv_freq = ROPE_THETA ** (-jnp.arange(n_freq, dtype=F32) / n_freq)
    ang_r = row_pos[:, None] * inv_freq
    ang_c = col_pos[:, None] * inv_freq
    ang = jnp.concatenate([ang_r, ang_r, ang_c, ang_c] * 2, axis=-1)
    first = (jnp.arange(LANES) % (2 * n_freq)) < n_freq
    sin = jnp.sin(ang)
    return jnp.cos(ang), jnp.where(first, -sin, 0.0), jnp.where(first, 0.0, sin)


def _attn_kernel(qt_ref, kc_ref, vtc_ref, kl_ref, vtl_ref, lam_ref, g_ref, o_ref,
                 qs_ref, s_ref, mx_ref, m_ref, acc_ref, grow_ref, *, tq, lambda_init):
    dim = lax.broadcasted_iota(jnp.int32, (HEAD_W, tq), 0)
    qt = qt_ref[...]
    zero = jnp.zeros_like(qt)
    qs_ref[:, 0:tq] = jnp.where(dim < SUB_HEAD, qt, zero)
    qs_ref[:, tq:2 * tq] = jnp.where(dim >= SUB_HEAD, qt, zero)
    n_chunks, _, tk = vtl_ref.shape
    n_ctx = kc_ref.shape[0]

    def lat_k(j):
        return kl_ref[pl.ds(pl.multiple_of(j * tk, tk), tk), :]

    def finish():
        lp = lam_ref[...]
        lam = (jnp.exp(jnp.sum(lp[0:1, :] * lp[1:2, :], axis=1, keepdims=True))
               - jnp.exp(jnp.sum(lp[2:3, :] * lp[3:4, :], axis=1, keepdims=True)) + lambda_init)
        ot = acc_ref[0:HEAD_W, :] / acc_ref[HEAD_W:HEAD_W + 1, :]
        od = ot[:, 0:tq] - lam * ot[:, tq:2 * tq]
        od = od * lax.rsqrt(jnp.mean(od * od, axis=0, keepdims=True) + EPS) * (g_ref[...] * (1.0 - lambda_init))
        o_ref[...] = od.T.astype(BF16)

    def single_pass_block(j):
        st = jnp.dot(lat_k(j), qs_ref[...], preferred_element_type=F32)
        m_prev = m_ref[...]
        pt = jnp.exp2(st - m_prev).astype(BF16)
        mx = jnp.max(st, axis=0, keepdims=True)
        m_new = jnp.maximum(m_prev, mx)
        grow_ref[...] = jnp.maximum(grow_ref[...], mx - m_prev)
        pv = jnp.dot(vtl_ref[j], pt, preferred_element_type=F32)
        acc_ref[...] = (acc_ref[...] + pv) * jnp.exp2(m_prev - m_new)
        m_ref[...] = m_new

    st = jnp.dot(kc_ref[...], qs_ref[...], preferred_element_type=F32)
    m_ref[...] = jnp.max(st, axis=0, keepdims=True)
    acc_ref[...] = jnp.dot(vtc_ref[0], jnp.exp2(st - m_ref[...]).astype(BF16), preferred_element_type=F32)
    grow_ref[...] = jnp.zeros_like(grow_ref)

    def single_pass_body(jj, carry):
        for t in range(PIPE_BODY):
            single_pass_block(PIPE_BODY * jj + t)
        return carry

    n_body = n_chunks // PIPE_BODY
    lax.fori_loop(0, n_body, single_pass_body, 0)
    for j in range(PIPE_BODY * n_body, n_chunks):
        single_pass_block(j)
    finish()

    @pl.when(jnp.max(grow_ref[...]) > MAX_GROWTH_LOG2)
    def _():
        m_ref[...] = jnp.full_like(m_ref, -jnp.inf)
        acc_ref[...] = jnp.zeros_like(acc_ref)

        def stage(slot, k):
            st = jnp.dot(k, qs_ref[...], preferred_element_type=F32)
            s_ref[slot, 0:k.shape[0], :] = st
            mx_ref[slot] = jnp.max(st, axis=0, keepdims=True)

        def consume(slot, n, vt):
            m_prev = m_ref[...]
            m_new = jnp.maximum(m_prev, mx_ref[slot])
            alpha = jnp.exp2(m_prev - m_new)
            pt = jnp.exp2(s_ref[slot, 0:n, :] - m_new).astype(BF16)
            acc_ref[...] = alpha * acc_ref[...] + jnp.dot(vt, pt, preferred_element_type=F32)
            m_ref[...] = m_new

        stage(1, kc_ref[...])
        stage(0, lat_k(0))
        consume(1, n_ctx, vtc_ref[0])

        def pair(jj, carry):
            j = 2 * jj + 1
            stage(1, lat_k(j))
            consume(0, tk, vtl_ref[j - 1])
            stage(0, lat_k(j + 1))
            consume(1, tk, vtl_ref[j])
            return carry

        n_pairs = (n_chunks - 1) // 2
        lax.fori_loop(0, n_pairs, pair, 0)
        slot = 1
        for j in range(2 * n_pairs + 1, n_chunks):
            stage(slot, lat_k(j))
            consume(1 - slot, tk, vtl_ref[j - 1])
            slot = 1 - slot
        consume(1 - slot, tk, vtl_ref[n_chunks - 1])
        finish()


def _attn(qt, k_ctx, vt_ctx, k_lat, vt_lat, lam_params, subln_g, lambda_init, tq):
    b, heads, _, n_lat = qt.shape
    n_ctx = k_ctx.shape[2]
    n_chunks, _, tk = vt_lat.shape[2:]
    assert n_ctx <= tk
    kern = functools.partial(_attn_kernel, tq=tq, lambda_init=lambda_init)
    vt_block = lambda a: pl.BlockSpec((None, None) + a.shape[2:], lambda b, h, i: (b, h, 0, 0, 0))
    return pl.pallas_call(
        kern,
        grid=(b, heads, n_lat // tq),
        in_specs=[pl.BlockSpec((None, None, HEAD_W, tq), lambda b, h, i: (b, h, 0, i)),
                  pl.BlockSpec((None, None, n_ctx, HEAD_W), lambda b, h, i: (b, h, 0, 0)),
                  vt_block(vt_ctx),
                  pl.BlockSpec((None, None, n_lat, HEAD_W), lambda b, h, i: (b, h, 0, 0)),
                  vt_block(vt_lat),
                  pl.BlockSpec(lam_params.shape, lambda b, h, i: (0, 0)),
                  pl.BlockSpec(subln_g.shape, lambda b, h, i: (0, 0))],
        out_specs=pl.BlockSpec((None, tq, HEAD_W), lambda b, h, i: (b, i, h)),
        out_shape=jax.ShapeDtypeStruct((b, n_lat, heads * HEAD_W), BF16),
        scratch_shapes=[pltpu.VMEM((HEAD_W, 2 * tq), BF16), pltpu.VMEM((2, tk, 2 * tq), F32),
                        pltpu.VMEM((2, 1, 2 * tq), F32), pltpu.VMEM((1, 2 * tq), F32),
                        pltpu.VMEM((VT_ROWS, 2 * tq), F32), pltpu.VMEM((1, 2 * tq), F32)],
        compiler_params=_params(3),
        name="attn",
    )(qt, k_ctx, vt_ctx, k_lat, vt_lat, lam_params, subln_g)


def _tile(l, target):
    return min(l, target)


def kernel(x, c, ctx, c_ctx, ada_w, ada_b, gm_w_in, gm_norm_g, gm_w_s, gm_b_s, gm_w_out, da_w_qkv, da_lambda_q1, da_lambda_k1, da_lambda_q2, da_lambda_k2, da_subln_g, da_w_out, ffn_w_up, ffn_conv_w, ffn_conv_b, ffn_w_down, final_norm_g):
    bsz, n_lat, d = x.shape
    depth = ada_w.shape[0]
    ffn_dim = ffn_w_down.shape[1]
    assert depth == 2 and bsz < ADA_ROWS and n_lat % GRID_W == 0
    ctx_row = bsz
    ffn_cw = 2 * LANES
    n_chunks = ffn_dim // ffn_cw
    assert n_chunks * ffn_cw == ffn_dim

    cond = jnp.zeros((ADA_ROWS, d), F32).at[:bsz].set(c).at[ctx_row].set(c_ctx)
    mod = _ada(cond, ada_w, ada_b).reshape(depth, ADA_ROWS, 6, d)

    def ffn_weights(i):
        w_up = ffn_w_up[i].astype(BF16).reshape(d, 2, n_chunks, ffn_cw).transpose(1, 2, 0, 3)
        cw = ffn_conv_w[i].reshape(CONV_W, 2, n_chunks, ffn_cw).transpose(1, 2, 0, 3)
        cb = ffn_conv_b[i].reshape(2, n_chunks, 1, ffn_cw)
        w_dn = ffn_w_down[i].astype(BF16).reshape(n_chunks, ffn_cw, d)
        return w_up, cw, cb, w_dn

    tm_lat = _tile(n_lat, 512)
    tm_ctx = _tile(ctx.shape[1], 512)

    gm = (gm_w_in[0].astype(BF16), gm_norm_g[0].reshape(1, -1), gm_w_s[0].astype(BF16),
          gm_b_s[0][:, :, None], gm_w_out[0].astype(BF16))
    fw = ffn_weights(0)
    h = _gmlp(x, mod[0], None, *gm, tm_lat)
    hc = _gmlp(ctx, mod[0], ctx_row, *gm, tm_ctx)
    h = _ffn(h, mod[0], None, *fw, None, _tile(n_lat, 1024))
    hc = _ffn(hc, mod[0], ctx_row, *fw, None, tm_ctx)

    lambda_init = 0.8 - 0.6 * math.exp(-0.3 * 1)
    w_qkv = da_w_qkv[0].astype(BF16)
    da_width = w_qkv.shape[1] // 3
    qt, k_lat, vt_lat = _qkv(h, mod[1], None, w_qkv, _rope_tables(n_lat), tm_lat)
    _, k_ctx, vt_ctx = _qkv(hc, mod[1], ctx_row, w_qkv[:, da_width:], None, tm_ctx)
    lam_params = jnp.stack([da_lambda_q1[0], da_lambda_k1[0], da_lambda_q2[0], da_lambda_k2[0]])
    o = _attn(qt, k_ctx, vt_ctx, k_lat, vt_lat, lam_params, da_subln_g[0].reshape(-1, 1), lambda_init, 1024)
    fw = ffn_weights(1)
    return _ffn(h, mod[1], None, *fw, final_norm_g.reshape(1, -1), _tile(n_lat, 1024),
                mixer=(o, da_w_out[0].astype(BF16)))
```

```python
import functools
import math

import jax
import jax.numpy as jnp
from jax import lax
from jax.experimental import pallas as pl
from jax.experimental.pallas import tpu as pltpu

EPS = 1e-6
GRID_W = 64
GM_GROUPS = 8
CHUNK = 128
SUB_HEAD = 64
HEAD_W = 2 * SUB_HEAD
VT_ROWS = HEAD_W + 16
ROPE_THETA = 10000.0
CONV_W = 3
HALO = 16
LANES = 128
ADA_ROWS = 16
PIPE_BODY = 16
MAX_GROWTH_LOG2 = 32.0
FFN_PIPE_BODY = 2
VMEM_LIMIT_BYTES = 60 * 1024 * 1024

F32 = jnp.float32
BF16 = jnp.bfloat16


def _rms(x):
    return x * lax.rsqrt(jnp.mean(x * x, axis=-1, keepdims=True) + EPS)


def _modulate(x, mod_ref, k):
    return _rms(x) * (1.0 + mod_ref[k + 1:k + 2, :]) + mod_ref[k:k + 1, :]


def _silu(x):
    return x * (1.0 / (1.0 + jnp.exp(-x)))


def _gelu_tanh(x):
    return 0.5 * x * (1.0 + jnp.tanh(math.sqrt(2.0 / math.pi) * (x + 0.044715 * (x * x * x))))


def _params(n_axes):
    return pltpu.CompilerParams(dimension_semantics=("arbitrary",) * n_axes,
                                vmem_limit_bytes=VMEM_LIMIT_BYTES)


def _resident(shape):
    nd = len(shape)
    return pl.BlockSpec(shape, lambda *_: (0,) * nd, pipeline_mode=pl.Buffered(1))


def _ada_kernel(cond_ref, w_ref, b_ref, out_ref):
    s = _silu(cond_ref[...])
    out_ref[...] = jnp.dot(s, w_ref[...], preferred_element_type=F32,
                           precision=lax.Precision.HIGHEST) + b_ref[...]


def _ada(cond, ada_w, ada_b):
    depth, d, n = ada_w.shape
    nb = 6 * LANES * 2
    return pl.pallas_call(
        _ada_kernel,
        grid=(depth, n // nb),
        in_specs=[pl.BlockSpec((ADA_ROWS, d), lambda l, j: (0, 0)),
                  pl.BlockSpec((None, d, nb), lambda l, j: (l, 0, j)),
                  pl.BlockSpec((None, 1, nb), lambda l, j: (l, 0, j))],
        out_specs=pl.BlockSpec((None, ADA_ROWS, nb), lambda l, j: (l, 0, j)),
        out_shape=jax.ShapeDtypeStruct((depth, ADA_ROWS, n), F32),
        compiler_params=_params(2),
        name="ada",
    )(cond, ada_w, ada_b.reshape(depth, 1, n))


def _mod_spec(d, ctx_row):
    if ctx_row is None:
        return pl.BlockSpec((None, 6, d), lambda b, i: (b, 0, 0))
    return pl.BlockSpec((None, 6, d), lambda b, i: (ctx_row, 0, 0))


def _gmlp_kernel(h_ref, mod_ref, win_ref, ng_ref, ws_ref, bs_ref, wout_ref, out_ref, vn_ref, uv_ref, *, tm, gw):
    x = h_ref[...]
    xl = _modulate(x, mod_ref, 0).astype(BF16)
    v = _gelu_tanh(jnp.dot(xl, win_ref[:, gw:], preferred_element_type=F32))
    vn_ref[...] = (_rms(v) * ng_ref[...]).astype(BF16)
    gd = gw // GM_GROUPS
    for g in range(GM_GROUPS):
        cols = slice(g * gd, (g + 1) * gd)
        u = _gelu_tanh(jnp.dot(xl, win_ref[:, cols], preferred_element_type=F32))
        for c in range(tm // CHUNK):
            rows = slice(c * CHUNK, (c + 1) * CHUNK)
            mix = jnp.dot(ws_ref[g], vn_ref[rows, cols], preferred_element_type=F32) + bs_ref[g]
            uv_ref[rows, cols] = (u[rows, :] * mix).astype(BF16)
    out = jnp.dot(uv_ref[...], wout_ref[...], preferred_element_type=F32)
    out_ref[...] = x + mod_ref[2:3, :] * out


def _gmlp(h, mod, ctx_row, w_in, norm_g, w_s, b_s, w_out, tm):
    b, l, d = h.shape
    gw = w_out.shape[0]
    kern = functools.partial(_gmlp_kernel, tm=tm, gw=gw)
    return pl.pallas_call(
        kern,
        grid=(b, l // tm),
        in_specs=[pl.BlockSpec((None, tm, d), lambda b, i: (b, i, 0)),
                  _mod_spec(d, ctx_row),
                  _resident(w_in.shape), _resident(norm_g.shape), _resident(w_s.shape),
                  _resident(b_s.shape), _resident(w_out.shape)],
        out_specs=pl.BlockSpec((None, tm, d), lambda b, i: (b, i, 0)),
        out_shape=jax.ShapeDtypeStruct(h.shape, F32),
        scratch_shapes=[pltpu.VMEM((tm, gw), BF16), pltpu.VMEM((tm, gw), BF16)],
        compiler_params=_params(2),
        name="gmlp",
    )(h, mod, w_in, norm_g, w_s, b_s, w_out)


def _ffn_kernel(*refs, tm, n_chunks, final, mixer):
    refs = list(refs)
    h_refs = refs[0:3]
    del refs[0:3]
    if mixer:
        o_refs, wproj_ref = refs[0:3], refs[3]
        del refs[0:4]
    mod_ref, wup_ref, cw_ref, cb_ref, wdn_ref = refs[0:5]
    del refs[0:5]
    if final:
        fg_ref = refs.pop(0)
    out_ref, xm_ref, acc_ref, z_ref = refs
    i = pl.program_id(1)
    last = pl.num_programs(1) - 1

    def stream(k):
        if not mixer:
            return h_refs[k][...]
        return h_refs[k][...] + mod_ref[2:3, :] * jnp.dot(o_refs[k][...], wproj_ref[...], preferred_element_type=F32)

    out_ref[...] = stream(0)
    xm_ref[0:HALO, :] = jnp.where(i == 0, 0.0, _modulate(stream(1), mod_ref, 3)).astype(BF16)
    xm_ref[HALO:HALO + tm, :] = _modulate(out_ref[...], mod_ref, 3).astype(BF16)
    xm_ref[HALO + tm:, :] = jnp.where(i == last, 0.0, _modulate(stream(2), mod_ref, 3)).astype(BF16)
    acc_ref[...] = jnp.zeros_like(acc_ref)

    def stage(slot, c):
        xm = xm_ref[...]
        for s in range(2):
            z_ref[slot, s] = jnp.dot(xm, wup_ref[s, c], preferred_element_type=F32)

    def consume(slot, c):
        def conv_half(s):
            cw = cw_ref[s, c]
            taps = [z_ref[slot, s, HALO - 1 + j:HALO - 1 + j + tm, :] * cw[j:j + 1, :] for j in range(CONV_W)]
            return taps[0] + taps[1] + taps[2] + cb_ref[s, c]

        act = (_silu(conv_half(0)) * conv_half(1)).astype(BF16)
        acc_ref[...] += jnp.dot(act, wdn_ref[c], preferred_element_type=F32)

    def body(jj, carry):
        c = FFN_PIPE_BODY * jj
        for t in range(FFN_PIPE_BODY):
            stage((t + 1) % 2, c + t + 1)
            consume(t % 2, c + t)
        return carry

    n_body = (n_chunks - 1) // FFN_PIPE_BODY
    stage(0, 0)
    lax.fori_loop(0, n_body, body, 0)
    slot = 0
    for c in range(FFN_PIPE_BODY * n_body, n_chunks - 1):
        stage(1 - slot, c + 1)
        consume(slot, c)
        slot = 1 - slot
    consume(slot, n_chunks - 1)
    y = out_ref[...] + mod_ref[5:6, :] * acc_ref[...]
    if final:
        y = _rms(y) * fg_ref[...]
    out_ref[...] = y


def _ffn(h, mod, ctx_row, w_up, conv_w, conv_b, w_down, final_g, tm, mixer=None):
    b, l, d = h.shape
    n_chunks = w_down.shape[0]
    nh = l // HALO
    per = tm // HALO
    final = final_g is not None
    kern = functools.partial(_ffn_kernel, tm=tm, n_chunks=n_chunks, final=final, mixer=mixer is not None)
    tile_specs = [pl.BlockSpec((None, tm, d), lambda b, i: (b, i, 0)),
                  pl.BlockSpec((None, HALO, d), lambda b, i: (b, jnp.maximum(i * per - 1, 0), 0)),
                  pl.BlockSpec((None, HALO, d), lambda b, i: (b, jnp.minimum((i + 1) * per, nh - 1), 0))]
    in_specs = list(tile_specs)
    args = [h, h, h]
    if mixer is not None:
        o, w_proj = mixer
        in_specs += tile_specs + [_resident(w_proj.shape)]
        args += [o, o, o, w_proj]
    in_specs += [_mod_spec(d, ctx_row),
                 _resident(w_up.shape), _resident(conv_w.shape), _resident(conv_b.shape), _resident(w_down.shape)]
    args += [mod, w_up, conv_w, conv_b, w_down]
    if final:
        in_specs.append(_resident(final_g.shape))
        args.append(final_g)
    return pl.pallas_call(
        kern,
        grid=(b, l // tm),
        in_specs=in_specs,
        out_specs=pl.BlockSpec((None, tm, d), lambda b, i: (b, i, 0)),
        out_shape=jax.ShapeDtypeStruct(h.shape, F32),
        scratch_shapes=[pltpu.VMEM((tm + 2 * HALO, d), BF16), pltpu.VMEM((tm, d), F32),
                        pltpu.VMEM((2, 2, tm + 2 * HALO, w_up.shape[3]), F32)],
        compiler_params=_params(2),
        name="ffn",
    )(*args)


def _qkv_kernel(*refs, heads, with_q):
    if with_q:
        h_ref, mod_ref, w_ref, cos_ref, sa_ref, sb_ref, qt_ref, k_ref, vt_ref = refs
    else:
        h_ref, mod_ref, w_ref, k_ref, vt_ref = refs
    xl = _modulate(h_ref[...], mod_ref, 0).astype(BF16)

    def rope(yb):
        return (yb * cos_ref[...] + pltpu.roll(yb, LANES - SUB_HEAD // 4, 1) * sa_ref[...]
                + pltpu.roll(yb, SUB_HEAD // 4, 1) * sb_ref[...])

    group = 4
    for part in range(3 if with_q else 2):
        kind = part if with_q else part + 1
        for h0 in range(0, heads, group):
            c0 = (part * heads + h0) * HEAD_W
            y = jnp.dot(xl, w_ref[:, c0:c0 + group * HEAD_W], preferred_element_type=F32)
            for hh in range(group):
                h = h0 + hh
                yb = y[:, hh * HEAD_W:(hh + 1) * HEAD_W]
                if kind == 0:
                    qt_ref[h] = (rope(yb) * (SUB_HEAD ** -0.5 * math.log2(math.e))).T.astype(BF16)
                elif kind == 1:
                    k_ref[h] = (rope(yb) if with_q else yb).astype(BF16)
                else:
                    vt_ref[h, 0:HEAD_W, :] = yb.T.astype(BF16)
                    vt_ref[h, HEAD_W:VT_ROWS, :] = jnp.ones((VT_ROWS - HEAD_W, yb.shape[0]), BF16)


def _qkv(h, mod, ctx_row, w, tables, tm):
    b, l, d = h.shape
    with_q = tables is not None
    heads = w.shape[1] // HEAD_W // (3 if with_q else 2)
    kern = functools.partial(_qkv_kernel, heads=heads, with_q=with_q)
    in_specs = [pl.BlockSpec((None, tm, d), lambda b, i: (b, i, 0)), _mod_spec(d, ctx_row), _resident(w.shape)]
    args = [h, mod, w]
    out_specs = [pl.BlockSpec((None, heads, tm, HEAD_W), lambda b, i: (b, 0, i, 0)),
                 pl.BlockSpec((None, heads, None, VT_ROWS, tm), lambda b, i: (b, 0, i, 0, 0))]
    out_shape = [jax.ShapeDtypeStruct((b, heads, l, HEAD_W), BF16),
                 jax.ShapeDtypeStruct((b, heads, l // tm, VT_ROWS, tm), BF16)]
    if with_q:
        in_specs += [pl.BlockSpec((tm, LANES), lambda b, i: (i, 0))] * 3
        args += list(tables)
        out_specs.insert(0, pl.BlockSpec((None, heads, HEAD_W, tm), lambda b, i: (b, 0, 0, i)))
        out_shape.insert(0, jax.ShapeDtypeStruct((b, heads, HEAD_W, l), BF16))
    outs = pl.pallas_call(
        kern,
        grid=(b, l // tm),
        in_specs=in_specs,
        out_specs=out_specs,
        out_shape=out_shape,
        compiler_params=_params(2),
        name="qkv",
    )(*args)
    return outs if with_q else (None, *outs)


def _rope_tables(n_lat):
    rows = n_lat // GRID_W
    row_pos = jnp.broadcast_to(jnp.arange(rows, dtype=F32)[:, None], (rows, GRID_W)).reshape(-1)
    col_pos = jnp.broadcast_to(jnp.arange(GRID_W, dtype=F32)[None, :], (rows, GRID_W)).reshape(-1)
    n_freq = SUB_HEAD // 4
    inv_freq = ROPE_THETA ** (-jnp.arange(n_freq, dtype=F32) / n_freq)
    ang_r = row_pos[:, None] * inv_freq
    ang_c = col_pos[:, None] * inv_freq
    ang = jnp.concatenate([ang_r, ang_r, ang_c, ang_c] * 2, axis=-1)
    first = (jnp.arange(LANES) % (2 * n_freq)) < n_freq
    sin = jnp.sin(ang)
    return jnp.cos(ang), jnp.where(first, -sin, 0.0), jnp.where(first, 0.0, sin)


def _attn_kernel(qt_ref, kc_ref, vtc_ref, kl_ref, vtl_ref, lam_ref, g_ref, o_ref,
                 qs_ref, s_ref, mx_ref, m_ref, acc_ref, grow_ref, *, tq, lambda_init):
    dim = lax.broadcasted_iota(jnp.int32, (HEAD_W, tq), 0)
    qt = qt_ref[...]
    zero = jnp.zeros_like(qt)
    qs_ref[:, 0:tq] = jnp.where(dim < SUB_HEAD, qt, zero)
    qs_ref[:, tq:2 * tq] = jnp.where(dim >= SUB_HEAD, qt, zero)
    n_chunks, _, tk = vtl_ref.shape
    n_ctx = kc_ref.shape[0]

    def lat_k(j):
        return kl_ref[pl.ds(pl.multiple_of(j * tk, tk), tk), :]

    def finish():
        lp = lam_ref[...]
        lam = (jnp.exp(jnp.sum(lp[0:1, :] * lp[1:2, :], axis=1, keepdims=True))
               - jnp.exp(jnp.sum(lp[2:3, :] * lp[3:4, :], axis=1, keepdims=True)) + lambda_init)
        ot = acc_ref[0:HEAD_W, :] / acc_ref[HEAD_W:HEAD_W + 1, :]
        od = ot[:, 0:tq] - lam * ot[:, tq:2 * tq]
        od = od * lax.rsqrt(jnp.mean(od * od, axis=0, keepdims=True) + EPS) * (g_ref[...] * (1.0 - lambda_init))
        o_ref[...] = od.T.astype(BF16)

    def single_pass_block(j):
        st = jnp.dot(lat_k(j), qs_ref[...], preferred_element_type=F32)
        m_prev = m_ref[...]
        pt = jnp.exp2(st - m_prev).astype(BF16)
        mx = jnp.max(st, axis=0, keepdims=True)
        m_new = jnp.maximum(m_prev, mx)
        grow_ref[...] = jnp.maximum(grow_ref[...], mx - m_prev)
        pv = jnp.dot(vtl_ref[j], pt, preferred_element_type=F32)
        acc_ref[...] = (acc_ref[...] + pv) * jnp.exp2(m_prev - m_new)
        m_ref[...] = m_new

    st = jnp.dot(kc_ref[...], qs_ref[...], preferred_element_type=F32)
    m_ref[...] = jnp.max(st, axis=0, keepdims=True)
    acc_ref[...] = jnp.dot(vtc_ref[0], jnp.exp2(st - m_ref[...]).astype(BF16), preferred_element_type=F32)
    grow_ref[...] = jnp.zeros_like(grow_ref)

    def single_pass_body(jj, carry):
        for t in range(PIPE_BODY):
            single_pass_block(PIPE_BODY * jj + t)
        return carry

    n_body = n_chunks // PIPE_BODY
    lax.fori_loop(0, n_body, single_pass_body, 0)
    for j in range(PIPE_BODY * n_body, n_chunks):
        single_pass_block(j)
    finish()

    @pl.when(jnp.max(grow_ref[...]) > MAX_GROWTH_LOG2)
    def _():
        m_ref[...] = jnp.full_like(m_ref, -jnp.inf)
        acc_ref[...] = jnp.zeros_like(acc_ref)

        def stage(slot, k):
            st = jnp.dot(k, qs_ref[...], preferred_element_type=F32)
            s_ref[slot, 0:k.shape[0], :] = st
            mx_ref[slot] = jnp.max(st, axis=0, keepdims=True)

        def consume(slot, n, vt):
            m_prev = m_ref[...]
            m_new = jnp.maximum(m_prev, mx_ref[slot])
            alpha = jnp.exp2(m_prev - m_new)
            pt = jnp.exp2(s_ref[slot, 0:n, :] - m_new).astype(BF16)
            acc_ref[...] = alpha * acc_ref[...] + jnp.dot(vt, pt, preferred_element_type=F32)
            m_ref[...] = m_new

        stage(1, kc_ref[...])
        stage(0, lat_k(0))
        consume(1, n_ctx, vtc_ref[0])

        def pair(jj, carry):
            j = 2 * jj + 1
            stage(1, lat_k(j))
            consume(0, tk, vtl_ref[j - 1])
            stage(0, lat_k(j + 1))
            consume(1, tk, vtl_ref[j])
            return carry

        n_pairs = (n_chunks - 1) // 2
        lax.fori_loop(0, n_pairs, pair, 0)
        slot = 1
        for j in range(2 * n_pairs + 1, n_chunks):
            stage(slot, lat_k(j))
            consume(1 - slot, tk, vtl_ref[j - 1])
            slot = 1 - slot
        consume(1 - slot, tk, vtl_ref[n_chunks - 1])
        finish()


def _attn(qt, k_ctx, vt_ctx, k_lat, vt_lat, lam_params, subln_g, lambda_init, tq):
    b, heads, _, n_lat = qt.shape
    n_ctx = k_ctx.shape[2]
    n_chunks, _, tk = vt_lat.shape[2:]
    assert n_ctx <= tk
    kern = functools.partial(_attn_kernel, tq=tq, lambda_init=lambda_init)
    vt_block = lambda a: pl.BlockSpec((None, None) + a.shape[2:], lambda b, h, i: (b, h, 0, 0, 0))
    return pl.pallas_call(
        kern,
        grid=(b, heads, n_lat // tq),
        in_specs=[pl.BlockSpec((None, None, HEAD_W, tq), lambda b, h, i: (b, h, 0, i)),
                  pl.BlockSpec((None, None, n_ctx, HEAD_W), lambda b, h, i: (b, h, 0, 0)),
                  vt_block(vt_ctx),
                  pl.BlockSpec((None, None, n_lat, HEAD_W), lambda b, h, i: (b, h, 0, 0)),
                  vt_block(vt_lat),
                  pl.BlockSpec(lam_params.shape, lambda b, h, i: (0, 0)),
                  pl.BlockSpec(subln_g.shape, lambda b, h, i: (0, 0))],
        out_specs=pl.BlockSpec((None, tq, HEAD_W), lambda b, h, i: (b, i, h)),
        out_shape=jax.ShapeDtypeStruct((b, n_lat, heads * HEAD_W), BF16),
        scratch_shapes=[pltpu.VMEM((HEAD_W, 2 * tq), BF16), pltpu.VMEM((2, tk, 2 * tq), F32),
                        pltpu.VMEM((2, 1, 2 * tq), F32), pltpu.VMEM((1, 2 * tq), F32),
                        pltpu.VMEM((VT_ROWS, 2 * tq), F32), pltpu.VMEM((1, 2 * tq), F32)],
        compiler_params=_params(3),
        name="attn",
    )(qt, k_ctx, vt_ctx, k_lat, vt_lat, lam_params, subln_g)


def _tile(l, target):
    return min(l, target)


def kernel(x, c, ctx, c_ctx, ada_w, ada_b, gm_w_in, gm_norm_g, gm_w_s, gm_b_s, gm_w_out, da_w_qkv, da_lambda_q1, da_lambda_k1, da_lambda_q2, da_lambda_k2, da_subln_g, da_w_out, ffn_w_up, ffn_conv_w, ffn_conv_b, ffn_w_down, final_norm_g):
    bsz, n_lat, d = x.shape
    depth = ada_w.shape[0]
    ffn_dim = ffn_w_down.shape[1]
    assert depth == 2 and bsz < ADA_ROWS and n_lat % GRID_W == 0
    ctx_row = bsz
    ffn_cw = 2 * LANES
    n_chunks = ffn_dim // ffn_cw
    assert n_chunks * ffn_cw == ffn_dim

    cond = jnp.zeros((ADA_ROWS, d), F32).at[:bsz].set(c).at[ctx_row].set(c_ctx)
    mod = _ada(cond, ada_w, ada_b).reshape(depth, ADA_ROWS, 6, d)

    def ffn_weights(i):
        w_up = ffn_w_up[i].astype(BF16).reshape(d, 2, n_chunks, ffn_cw).transpose(1, 2, 0, 3)
        cw = ffn_conv_w[i].reshape(CONV_W, 2, n_chunks, ffn_cw).transpose(1, 2, 0, 3)
        cb = ffn_conv_b[i].reshape(2, n_chunks, 1, ffn_cw)
        w_dn = ffn_w_down[i].astype(BF16).reshape(n_chunks, ffn_cw, d)
        return w_up, cw, cb, w_dn

    tm_lat = _tile(n_lat, 1024)
    tm_ctx = _tile(ctx.shape[1], 512)

    gm = (gm_w_in[0].astype(BF16), gm_norm_g[0].reshape(1, -1), gm_w_s[0].astype(BF16),
          gm_b_s[0][:, :, None], gm_w_out[0].astype(BF16))
    fw = ffn_weights(0)
    h = _gmlp(x, mod[0], None, *gm, tm_lat)
    hc = _gmlp(ctx, mod[0], ctx_row, *gm, tm_ctx)
    h = _ffn(h, mod[0], None, *fw, None, _tile(n_lat, 1024))
    hc = _ffn(hc, mod[0], ctx_row, *fw, None, tm_ctx)

    lambda_init = 0.8 - 0.6 * math.exp(-0.3 * 1)
    w_qkv = da_w_qkv[0].astype(BF16)
    da_width = w_qkv.shape[1] // 3
    qt, k_lat, vt_lat = _qkv(h, mod[1], None, w_qkv, _rope_tables(n_lat), tm_lat)
    _, k_ctx, vt_ctx = _qkv(hc, mod[1], ctx_row, w_qkv[:, da_width:], None, tm_ctx)
    lam_params = jnp.stack([da_lambda_q1[0], da_lambda_k1[0], da_lambda_q2[0], da_lambda_k2[0]])
    o = _attn(qt, k_ctx, vt_ctx, k_lat, vt_lat, lam_params, da_subln_g[0].reshape(-1, 1), lambda_init, 1024)
    fw = ffn_weights(1)
    return _ffn(h, mod[1], None, *fw, final_norm_g.reshape(1, -1), _tile(n_lat, 1024),
                mixer=(o, da_w_out[0].astype(BF16)))
```

```python
import functools
import math

import jax
import jax.numpy as jnp
from jax import lax
from jax.experimental import pallas as pl
from jax.experimental.pallas import tpu as pltpu

EPS = 1e-6
GRID_W = 64
GM_GROUPS = 8
CHUNK = 128
SUB_HEAD = 64
HEAD_W = 2 * SUB_HEAD
VT_ROWS = HEAD_W + 16
ROPE_THETA = 10000.0
CONV_W = 3
HALO = 16
LANES = 128
ADA_ROWS = 16
PIPE_BODY = 16
TOKEN_TILE = 1024
QUERY_TILE = 2048
MAX_GROWTH_LOG2 = 32.0
FFN_PIPE_BODY = 2
VMEM_LIMIT_BYTES = 60 * 1024 * 1024

F32 = jnp.float32
BF16 = jnp.bfloat16


def _rms(x):
    return x * lax.rsqrt(jnp.mean(x * x, axis=-1, keepdims=True) + EPS)


def _modulate(x, mod_ref, k):
    return _rms(x) * (1.0 + mod_ref[k + 1:k + 2, :]) + mod_ref[k:k + 1, :]


def _silu(x):
    return x * (1.0 / (1.0 + jnp.exp(-x)))


def _gelu_tanh(x):
    return 0.5 * x * (1.0 + jnp.tanh(math.sqrt(2.0 / math.pi) * (x + 0.044715 * (x * x * x))))


def _params(n_axes):
    return pltpu.CompilerParams(dimension_semantics=("arbitrary",) * n_axes,
                                vmem_limit_bytes=VMEM_LIMIT_BYTES)


def _resident(shape):
    nd = len(shape)
    return pl.BlockSpec(shape, lambda *_: (0,) * nd, pipeline_mode=pl.Buffered(1))


def _ada_kernel(cond_ref, w_ref, b_ref, out_ref):
    s = _silu(cond_ref[...])
    out_ref[...] = jnp.dot(s, w_ref[...], preferred_element_type=F32,
                           precision=lax.Precision.HIGHEST) + b_ref[...]


def _ada(cond, ada_w, ada_b):
    depth, d, n = ada_w.shape
    nb = 6 * LANES * 2
    return pl.pallas_call(
        _ada_kernel,
        grid=(depth, n // nb),
        in_specs=[pl.BlockSpec((ADA_ROWS, d), lambda l, j: (0, 0)),
                  pl.BlockSpec((None, d, nb), lambda l, j: (l, 0, j)),
                  pl.BlockSpec((None, 1, nb), lambda l, j: (l, 0, j))],
        out_specs=pl.BlockSpec((None, ADA_ROWS, nb), lambda l, j: (l, 0, j)),
        out_shape=jax.ShapeDtypeStruct((depth, ADA_ROWS, n), F32),
        compiler_params=_params(2),
        name="ada",
    )(cond, ada_w, ada_b.reshape(depth, 1, n))


def _mod_spec(d, ctx_row):
    if ctx_row is None:
        return pl.BlockSpec((None, 6, d), lambda b, i: (b, 0, 0))
    return pl.BlockSpec((None, 6, d), lambda b, i: (ctx_row, 0, 0))


def _gmlp_kernel(h_ref, mod_ref, win_ref, ng_ref, ws_ref, bs_ref, wout_ref, out_ref, vn_ref, uv_ref, *, tm, gw):
    x = h_ref[...]
    xl = _modulate(x, mod_ref, 0).astype(BF16)
    v = _gelu_tanh(jnp.dot(xl, win_ref[:, gw:], preferred_element_type=F32))
    vn_ref[...] = (_rms(v) * ng_ref[...]).astype(BF16)
    gd = gw // GM_GROUPS
    for g in range(GM_GROUPS):
        cols = slice(g * gd, (g + 1) * gd)
        u = _gelu_tanh(jnp.dot(xl, win_ref[:, cols], preferred_element_type=F32))
        for c in range(tm // CHUNK):
            rows = slice(c * CHUNK, (c + 1) * CHUNK)
            mix = jnp.dot(ws_ref[g], vn_ref[rows, cols], preferred_element_type=F32) + bs_ref[g]
            uv_ref[rows, cols] = (u[rows, :] * mix).astype(BF16)
    out = jnp.dot(uv_ref[...], wout_ref[...], preferred_element_type=F32)
    out_ref[...] = x + mod_ref[2:3, :] * out


def _gmlp(h, mod, ctx_row, w_in, norm_g, w_s, b_s, w_out, tm):
    b, l, d = h.shape
    gw = w_out.shape[0]
    kern = functools.partial(_gmlp_kernel, tm=tm, gw=gw)
    return pl.pallas_call(
        kern,
        grid=(b, l // tm),
        in_specs=[pl.BlockSpec((None, tm, d), lambda b, i: (b, i, 0)),
                  _mod_spec(d, ctx_row),
                  _resident(w_in.shape), _resident(norm_g.shape), _resident(w_s.shape),
                  _resident(b_s.shape), _resident(w_out.shape)],
        out_specs=pl.BlockSpec((None, tm, d), lambda b, i: (b, i, 0)),
        out_shape=jax.ShapeDtypeStruct(h.shape, F32),
        scratch_shapes=[pltpu.VMEM((tm, gw), BF16), pltpu.VMEM((tm, gw), BF16)],
        compiler_params=_params(2),
        name="gmlp",
    )(h, mod, w_in, norm_g, w_s, b_s, w_out)


def _ffn_kernel(*refs, tm, n_chunks, final, mixer):
    refs = list(refs)
    h_refs = refs[0:3]
    del refs[0:3]
    if mixer:
        o_refs, wproj_ref = refs[0:3], refs[3]
        del refs[0:4]
    mod_ref, wup_ref, cw_ref, cb_ref, wdn_ref = refs[0:5]
    del refs[0:5]
    if final:
        fg_ref = refs.pop(0)
    out_ref, xm_ref, acc_ref, z_ref = refs
    i = pl.program_id(1)
    last = pl.num_programs(1) - 1

    def stream(k):
        if not mixer:
            return h_refs[k][...]
        return h_refs[k][...] + mod_ref[2:3, :] * jnp.dot(o_refs[k][...], wproj_ref[...], preferred_element_type=F32)

    out_ref[...] = stream(0)
    xm_ref[0:HALO, :] = jnp.where(i == 0, 0.0, _modulate(stream(1), mod_ref, 3)).astype(BF16)
    xm_ref[HALO:HALO + tm, :] = _modulate(out_ref[...], mod_ref, 3).astype(BF16)
    xm_ref[HALO + tm:, :] = jnp.where(i == last, 0.0, _modulate(stream(2), mod_ref, 3)).astype(BF16)
    acc_ref[...] = jnp.zeros_like(acc_ref)

    def stage(slot, c):
        xm = xm_ref[...]
        for s in range(2):
            z_ref[slot, s] = jnp.dot(xm, wup_ref[s, c], preferred_element_type=F32)

    def consume(slot, c):
        def conv_half(s):
            cw = cw_ref[s, c]
            taps = [z_ref[slot, s, HALO - 1 + j:HALO - 1 + j + tm, :] * cw[j:j + 1, :] for j in range(CONV_W)]
            return taps[0] + taps[1] + taps[2] + cb_ref[s, c]

        act = (_silu(conv_half(0)) * conv_half(1)).astype(BF16)
        acc_ref[...] += jnp.dot(act, wdn_ref[c], preferred_element_type=F32)

    def body(jj, carry):
        c = FFN_PIPE_BODY * jj
        for t in range(FFN_PIPE_BODY):
            stage((t + 1) % 2, c + t + 1)
            consume(t % 2, c + t)
        return carry

    n_body = (n_chunks - 1) // FFN_PIPE_BODY
    stage(0, 0)
    lax.fori_loop(0, n_body, body, 0)
    slot = 0
    for c in range(FFN_PIPE_BODY * n_body, n_chunks - 1):
        stage(1 - slot, c + 1)
        consume(slot, c)
        slot = 1 - slot
    consume(slot, n_chunks - 1)
    y = out_ref[...] + mod_ref[5:6, :] * acc_ref[...]
    if final:
        y = _rms(y) * fg_ref[...]
    out_ref[...] = y


def _ffn(h, mod, ctx_row, w_up, conv_w, conv_b, w_down, final_g, tm, mixer=None):
    b, l, d = h.shape
    n_chunks = w_down.shape[0]
    nh = l // HALO
    per = tm // HALO
    final = final_g is not None
    kern = functools.partial(_ffn_kernel, tm=tm, n_chunks=n_chunks, final=final, mixer=mixer is not None)
    tile_specs = [pl.BlockSpec((None, tm, d), lambda b, i: (b, i, 0)),
                  pl.BlockSpec((None, HALO, d), lambda b, i: (b, jnp.maximum(i * per - 1, 0), 0)),
                  pl.BlockSpec((None, HALO, d), lambda b, i: (b, jnp.minimum((i + 1) * per, nh - 1), 0))]
    in_specs = list(tile_specs)
    args = [h, h, h]
    if mixer is not None:
        o, w_proj = mixer
        in_specs += tile_specs + [_resident(w_proj.shape)]
        args += [o, o, o, w_proj]
    in_specs += [_mod_spec(d, ctx_row),
                 _resident(w_up.shape), _resident(conv_w.shape), _resident(conv_b.shape), _resident(w_down.shape)]
    args += [mod, w_up, conv_w, conv_b, w_down]
    if final:
        in_specs.append(_resident(final_g.shape))
        args.append(final_g)
    return pl.pallas_call(
        kern,
        grid=(b, l // tm),
        in_specs=in_specs,
        out_specs=pl.BlockSpec((None, tm, d), lambda b, i: (b, i, 0)),
        out_shape=jax.ShapeDtypeStruct(h.shape, F32),
        scratch_shapes=[pltpu.VMEM((tm + 2 * HALO, d), BF16), pltpu.VMEM((tm, d), F32),
                        pltpu.VMEM((2, 2, tm + 2 * HALO, w_up.shape[3]), F32)],
        compiler_params=_params(2),
        name="ffn",
    )(*args)


def _qkv_kernel(*refs, heads, with_q):
    if with_q:
        h_ref, mod_ref, w_ref, cos_ref, sa_ref, sb_ref, qt_ref, k_ref, vt_ref = refs
    else:
        h_ref, mod_ref, w_ref, k_ref, vt_ref = refs
    xl = _modulate(h_ref[...], mod_ref, 0).astype(BF16)

    def rope(yb):
        return (yb * cos_ref[...] + pltpu.roll(yb, LANES - SUB_HEAD // 4, 1) * sa_ref[...]
                + pltpu.roll(yb, SUB_HEAD // 4, 1) * sb_ref[...])

    group = 4
    for part in range(3 if with_q else 2):
        kind = part if with_q else part + 1
        for h0 in range(0, heads, group):
            c0 = (part * heads + h0) * HEAD_W
            y = jnp.dot(xl, w_ref[:, c0:c0 + group * HEAD_W], preferred_element_type=F32)
            for hh in range(group):
                h = h0 + hh
                yb = y[:, hh * HEAD_W:(hh + 1) * HEAD_W]
                if kind == 0:
                    qt_ref[h] = (rope(yb) * (SUB_HEAD ** -0.5 * math.log2(math.e))).T.astype(BF16)
                elif kind == 1:
                    k_ref[h] = (rope(yb) if with_q else yb).astype(BF16)
                else:
                    vt_ref[h, 0:HEAD_W, :] = yb.T.astype(BF16)
                    vt_ref[h, HEAD_W:VT_ROWS, :] = jnp.ones((VT_ROWS - HEAD_W, yb.shape[0]), BF16)


def _qkv(h, mod, ctx_row, w, tables, tm):
    b, l, d = h.shape
    with_q = tables is not None
    heads = w.shape[1] // HEAD_W // (3 if with_q else 2)
    kern = functools.partial(_qkv_kernel, heads=heads, with_q=with_q)
    in_specs = [pl.BlockSpec((None, tm, d), lambda b, i: (b, i, 0)), _mod_spec(d, ctx_row), _resident(w.shape)]
    args = [h, mod, w]
    out_specs = [pl.BlockSpec((None, heads, tm, HEAD_W), lambda b, i: (b, 0, i, 0)),
                 pl.BlockSpec((None, heads, None, VT_ROWS, tm), lambda b, i: (b, 0, i, 0, 0))]
    out_shape = [jax.ShapeDtypeStruct((b, heads, l, HEAD_W), BF16),
                 jax.ShapeDtypeStruct((b, heads, l // tm, VT_ROWS, tm), BF16)]
    if with_q:
        in_specs += [pl.BlockSpec((tm, LANES), lambda b, i: (i, 0))] * 3
        args += list(tables)
        out_specs.insert(0, pl.BlockSpec((None, heads, HEAD_W, tm), lambda b, i: (b, 0, 0, i)))
        out_shape.insert(0, jax.ShapeDtypeStruct((b, heads, HEAD_W, l), BF16))
    outs = pl.pallas_call(
        kern,
        grid=(b, l // tm),
        in_specs=in_specs,
        out_specs=out_specs,
        out_shape=out_shape,
        compiler_params=_params(2),
        name="qkv",
    )(*args)
    return outs if with_q else (None, *outs)


def _rope_tables(n_lat):
    rows = n_lat // GRID_W
    row_pos = jnp.broadcast_to(jnp.arange(rows, dtype=F32)[:, None], (rows, GRID_W)).reshape(-1)
    col_pos = jnp.broadcast_to(jnp.arange(GRID_W, dtype=F32)[None, :], (rows, GRID_W)).reshape(-1)
    n_freq = SUB_HEAD // 4
    inv_freq = ROPE_THETA ** (-jnp.arange(n_freq, dtype=F32) / n_freq)
    ang_r = row_pos[:, None] * inv_freq
    ang_c = col_pos[:, None] * inv_freq
    ang = jnp.concatenate([ang_r, ang_r, ang_c, ang_c] * 2, axis=-1)
    first = (jnp.arange(LANES) % (2 * n_freq)) < n_freq
    sin = jnp.sin(ang)
    return jnp.cos(ang), jnp.where(first, -sin, 0.0), jnp.where(first, 0.0, sin)


def _attn_kernel(qt_ref, kc_ref, vtc_ref, kl_ref, vtl_ref, lam_ref, g_ref, o_ref,
                 qs_ref, m_ref, acc_ref, grow_ref, *, tq, lambda_init):
    dim = lax.broadcasted_iota(jnp.int32, (HEAD_W, tq), 0)
    qt = qt_ref[...]
    zero = jnp.zeros_like(qt)
    qs_ref[:, 0:tq] = jnp.where(dim < SUB_HEAD, qt, zero)
    qs_ref[:, tq:2 * tq] = jnp.where(dim >= SUB_HEAD, qt, zero)
    n_chunks, _, tk = vtl_ref.shape

    def lat_k(j):
        return kl_ref[pl.ds(pl.multiple_of(j * tk, tk), tk), :]

    def finish():
        lp = lam_ref[...]
        lam = (jnp.exp(jnp.sum(lp[0:1, :] * lp[1:2, :], axis=1, keepdims=True))
               - jnp.exp(jnp.sum(lp[2:3, :] * lp[3:4, :], axis=1, keepdims=True)) + lambda_init)
        ot = acc_ref[0:HEAD_W, :] / acc_ref[HEAD_W:HEAD_W + 1, :]
        od = ot[:, 0:tq] - lam * ot[:, tq:2 * tq]
        od = od * lax.rsqrt(jnp.mean(od * od, axis=0, keepdims=True) + EPS) * (g_ref[...] * (1.0 - lambda_init))
        o_ref[...] = od.T.astype(BF16)

    def single_pass_block(j):
        st = jnp.dot(lat_k(j), qs_ref[...], preferred_element_type=F32)
        m_prev = m_ref[...]
        pt = jnp.exp2(st - m_prev).astype(BF16)
        mx = jnp.max(st, axis=0, keepdims=True)
        m_new = jnp.maximum(m_prev, mx)
        grow_ref[...] = jnp.maximum(grow_ref[...], mx - m_prev)
        pv = jnp.dot(vtl_ref[j], pt, preferred_element_type=F32)
        acc_ref[...] = (acc_ref[...] + pv) * jnp.exp2(m_prev - m_new)
        m_ref[...] = m_new

    st = jnp.dot(kc_ref[...], qs_ref[...], preferred_element_type=F32)
    m_ref[...] = jnp.max(st, axis=0, keepdims=True)
    acc_ref[...] = jnp.dot(vtc_ref[0], jnp.exp2(st - m_ref[...]).astype(BF16), preferred_element_type=F32)
    grow_ref[...] = jnp.zeros_like(grow_ref)

    def single_pass_body(jj, carry):
        for t in range(PIPE_BODY):
            single_pass_block(PIPE_BODY * jj + t)
        return carry

    n_body = n_chunks // PIPE_BODY
    lax.fori_loop(0, n_body, single_pass_body, 0)
    for j in range(PIPE_BODY * n_body, n_chunks):
        single_pass_block(j)
    finish()

    @pl.when(jnp.max(grow_ref[...]) > MAX_GROWTH_LOG2)
    def _():
        m_ref[...] = jnp.full_like(m_ref, -jnp.inf)
        acc_ref[...] = jnp.zeros_like(acc_ref)

        def two_phase_block(k, vt):
            st = jnp.dot(k, qs_ref[...], preferred_element_type=F32)
            m_prev = m_ref[...]
            m_new = jnp.maximum(m_prev, jnp.max(st, axis=0, keepdims=True))
            pt = jnp.exp2(st - m_new).astype(BF16)
            acc_ref[...] = (jnp.exp2(m_prev - m_new) * acc_ref[...]
                            + jnp.dot(vt, pt, preferred_element_type=F32))
            m_ref[...] = m_new

        two_phase_block(kc_ref[...], vtc_ref[0])

        def two_phase_body(j, carry):
            two_phase_block(lat_k(j), vtl_ref[j])
            return carry

        lax.fori_loop(0, n_chunks, two_phase_body, 0)
        finish()


def _attn(qt, k_ctx, vt_ctx, k_lat, vt_lat, lam_params, subln_g, lambda_init, tq):
    b, heads, _, n_lat = qt.shape
    n_ctx = k_ctx.shape[2]
    kern = functools.partial(_attn_kernel, tq=tq, lambda_init=lambda_init)
    vt_block = lambda a: pl.BlockSpec((None, None) + a.shape[2:], lambda b, h, i: (b, h, 0, 0, 0))
    return pl.pallas_call(
        kern,
        grid=(b, heads, n_lat // tq),
        in_specs=[pl.BlockSpec((None, None, HEAD_W, tq), lambda b, h, i: (b, h, 0, i)),
                  pl.BlockSpec((None, None, n_ctx, HEAD_W), lambda b, h, i: (b, h, 0, 0)),
                  vt_block(vt_ctx),
                  pl.BlockSpec((None, None, n_lat, HEAD_W), lambda b, h, i: (b, h, 0, 0)),
                  vt_block(vt_lat),
                  pl.BlockSpec(lam_params.shape, lambda b, h, i: (0, 0)),
                  pl.BlockSpec(subln_g.shape, lambda b, h, i: (0, 0))],
        out_specs=pl.BlockSpec((None, tq, HEAD_W), lambda b, h, i: (b, i, h)),
        out_shape=jax.ShapeDtypeStruct((b, n_lat, heads * HEAD_W), BF16),
        scratch_shapes=[pltpu.VMEM((HEAD_W, 2 * tq), BF16), pltpu.VMEM((1, 2 * tq), F32),
                        pltpu.VMEM((VT_ROWS, 2 * tq), F32), pltpu.VMEM((1, 2 * tq), F32)],
        compiler_params=_params(3),
        name="attn",
    )(qt, k_ctx, vt_ctx, k_lat, vt_lat, lam_params, subln_g)


def _tile(l, target):
    return min(l, target)


def kernel(x, c, ctx, c_ctx, ada_w, ada_b, gm_w_in, gm_norm_g, gm_w_s, gm_b_s, gm_w_out, da_w_qkv, da_lambda_q1, da_lambda_k1, da_lambda_q2, da_lambda_k2, da_subln_g, da_w_out, ffn_w_up, ffn_conv_w, ffn_conv_b, ffn_w_down, final_norm_g):
    bsz, n_lat, d = x.shape
    depth = ada_w.shape[0]
    ffn_dim = ffn_w_down.shape[1]
    assert depth == 2 and bsz < ADA_ROWS and n_lat % GRID_W == 0
    ctx_row = bsz
    ffn_cw = 2 * LANES
    n_chunks = ffn_dim // ffn_cw
    assert n_chunks * ffn_cw == ffn_dim

    cond = jnp.zeros((ADA_ROWS, d), F32).at[:bsz].set(c).at[ctx_row].set(c_ctx)
    mod = _ada(cond, ada_w, ada_b).reshape(depth, ADA_ROWS, 6, d)

    def ffn_weights(i):
        w_up = ffn_w_up[i].astype(BF16).reshape(d, 2, n_chunks, ffn_cw).transpose(1, 2, 0, 3)
        cw = ffn_conv_w[i].reshape(CONV_W, 2, n_chunks, ffn_cw).transpose(1, 2, 0, 3)
        cb = ffn_conv_b[i].reshape(2, n_chunks, 1, ffn_cw)
        w_dn = ffn_w_down[i].astype(BF16).reshape(n_chunks, ffn_cw, d)
        return w_up, cw, cb, w_dn

    tm_lat = _tile(n_lat, TOKEN_TILE)
    tm_ctx = _tile(ctx.shape[1], TOKEN_TILE)

    gm = (gm_w_in[0].astype(BF16), gm_norm_g[0].reshape(1, -1), gm_w_s[0].astype(BF16),
          gm_b_s[0][:, :, None], gm_w_out[0].astype(BF16))
    fw = ffn_weights(0)
    h = _gmlp(x, mod[0], None, *gm, tm_lat)
    hc = _gmlp(ctx, mod[0], ctx_row, *gm, tm_ctx)
    h = _ffn(h, mod[0], None, *fw, None, tm_lat)
    hc = _ffn(hc, mod[0], ctx_row, *fw, None, tm_ctx)

    lambda_init = 0.8 - 0.6 * math.exp(-0.3 * 1)
    w_qkv = da_w_qkv[0].astype(BF16)
    da_width = w_qkv.shape[1] // 3
    qt, k_lat, vt_lat = _qkv(h, mod[1], None, w_qkv, _rope_tables(n_lat), tm_lat)
    _, k_ctx, vt_ctx = _qkv(hc, mod[1], ctx_row, w_qkv[:, da_width:], None, tm_ctx)
    lam_params = jnp.stack([da_lambda_q1[0], da_lambda_k1[0], da_lambda_q2[0], da_lambda_k2[0]])
    o = _attn(qt, k_ctx, vt_ctx, k_lat, vt_lat, lam_params, da_subln_g[0].reshape(-1, 1), lambda_init,
              _tile(n_lat, QUERY_TILE))
    fw = ffn_weights(1)
    return _ffn(h, mod[1], None, *fw, final_norm_g.reshape(1, -1), tm_lat, mixer=(o, da_w_out[0].astype(BF16)))
```

```python
import functools
import math

import jax
import jax.numpy as jnp
from jax import lax
from jax.experimental import pallas as pl
from jax.experimental.pallas import tpu as pltpu

EPS = 1e-6
GRID_W = 64
GM_GROUPS = 8
CHUNK = 128
SUB_HEAD = 64
HEAD_W = 2 * SUB_HEAD
VT_ROWS = HEAD_W + 16
ROPE_THETA = 10000.0
CONV_W = 3
HALO = 16
LANES = 128
ADA_ROWS = 16
PIPE_BODY = 16
TOKEN_TILE = 1024
QUERY_TILE = 1024
MAX_GROWTH_LOG2 = 32.0
FFN_PIPE_BODY = 2
VMEM_LIMIT_BYTES = 60 * 1024 * 1024

F32 = jnp.float32
BF16 = jnp.bfloat16


def _rms(x):
    return x * lax.rsqrt(jnp.mean(x * x, axis=-1, keepdims=True) + EPS)


def _modulate(x, mod_ref, k):
    return _rms(x) * (1.0 + mod_ref[k + 1:k + 2, :]) + mod_ref[k:k + 1, :]


def _silu(x):
    return x * (1.0 / (1.0 + jnp.exp(-x)))


def _gelu_tanh(x):
    return 0.5 * x * (1.0 + jnp.tanh(math.sqrt(2.0 / math.pi) * (x + 0.044715 * (x * x * x))))


def _params(n_axes):
    return pltpu.CompilerParams(dimension_semantics=("arbitrary",) * n_axes,
                                vmem_limit_bytes=VMEM_LIMIT_BYTES)


def _resident(shape):
    nd = len(shape)
    return pl.BlockSpec(shape, lambda *_: (0,) * nd, pipeline_mode=pl.Buffered(1))


def _ada_kernel(cond_ref, w_ref, b_ref, out_ref):
    s = _silu(cond_ref[...])
    out_ref[...] = jnp.dot(s, w_ref[...], preferred_element_type=F32,
                           precision=lax.Precision.HIGHEST) + b_ref[...]


def _ada(cond, ada_w, ada_b):
    depth, d, n = ada_w.shape
    nb = 6 * LANES * 2
    return pl.pallas_call(
        _ada_kernel,
        grid=(depth, n // nb),
        in_specs=[pl.BlockSpec((ADA_ROWS, d), lambda l, j: (0, 0)),
                  pl.BlockSpec((None, d, nb), lambda l, j: (l, 0, j)),
                  pl.BlockSpec((None, 1, nb), lambda l, j: (l, 0, j))],
        out_specs=pl.BlockSpec((None, ADA_ROWS, nb), lambda l, j: (l, 0, j)),
        out_shape=jax.ShapeDtypeStruct((depth, ADA_ROWS, n), F32),
        compiler_params=_params(2),
        name="ada",
    )(cond, ada_w, ada_b.reshape(depth, 1, n))


def _mod_spec(d, ctx_row):
    if ctx_row is None:
        return pl.BlockSpec((None, 6, d), lambda b, i: (b, 0, 0))
    return pl.BlockSpec((None, 6, d), lambda b, i: (ctx_row, 0, 0))


def _gmlp_kernel(h_ref, mod_ref, win_ref, ng_ref, ws_ref, bs_ref, wout_ref, out_ref, vn_ref, uv_ref, *, tm, gw):
    x = h_ref[...]
    xl = _modulate(x, mod_ref, 0).astype(BF16)
    v = _gelu_tanh(jnp.dot(xl, win_ref[:, gw:], preferred_element_type=F32))
    vn_ref[...] = (_rms(v) * ng_ref[...]).astype(BF16)
    gd = gw // GM_GROUPS
    for g in range(GM_GROUPS):
        cols = slice(g * gd, (g + 1) * gd)
        u = _gelu_tanh(jnp.dot(xl, win_ref[:, cols], preferred_element_type=F32))
        for c in range(tm // CHUNK):
            rows = slice(c * CHUNK, (c + 1) * CHUNK)
            mix = jnp.dot(ws_ref[g], vn_ref[rows, cols], preferred_element_type=F32) + bs_ref[g]
            uv_ref[rows, cols] = (u[rows, :] * mix).astype(BF16)
    out = jnp.dot(uv_ref[...], wout_ref[...], preferred_element_type=F32)
    out_ref[...] = x + mod_ref[2:3, :] * out


def _gmlp(h, mod, ctx_row, w_in, norm_g, w_s, b_s, w_out, tm):
    b, l, d = h.shape
    gw = w_out.shape[0]
    kern = functools.partial(_gmlp_kernel, tm=tm, gw=gw)
    return pl.pallas_call(
        kern,
        grid=(b, l // tm),
        in_specs=[pl.BlockSpec((None, tm, d), lambda b, i: (b, i, 0)),
                  _mod_spec(d, ctx_row),
                  _resident(w_in.shape), _resident(norm_g.shape), _resident(w_s.shape),
                  _resident(b_s.shape), _resident(w_out.shape)],
        out_specs=pl.BlockSpec((None, tm, d), lambda b, i: (b, i, 0)),
        out_shape=jax.ShapeDtypeStruct(h.shape, F32),
        scratch_shapes=[pltpu.VMEM((tm, gw), BF16), pltpu.VMEM((tm, gw), BF16)],
        compiler_params=_params(2),
        name="gmlp",
    )(h, mod, w_in, norm_g, w_s, b_s, w_out)


def _ffn_kernel(*refs, tm, n_chunks, final, mixer):
    refs = list(refs)
    h_refs = refs[0:3]
    del refs[0:3]
    if mixer:
        o_refs, wproj_ref = refs[0:3], refs[3]
        del refs[0:4]
    mod_ref, wup_ref, cw_ref, cb_ref, wdn_ref = refs[0:5]
    del refs[0:5]
    if final:
        fg_ref = refs.pop(0)
    out_ref, xm_ref, acc_ref, z_ref = refs
    i = pl.program_id(1)
    last = pl.num_programs(1) - 1

    def stream(k):
        if not mixer:
            return h_refs[k][...]
        return h_refs[k][...] + mod_ref[2:3, :] * jnp.dot(o_refs[k][...], wproj_ref[...], preferred_element_type=F32)

    out_ref[...] = stream(0)
    xm_ref[0:HALO, :] = jnp.where(i == 0, 0.0, _modulate(stream(1), mod_ref, 3)).astype(BF16)
    xm_ref[HALO:HALO + tm, :] = _modulate(out_ref[...], mod_ref, 3).astype(BF16)
    xm_ref[HALO + tm:, :] = jnp.where(i == last, 0.0, _modulate(stream(2), mod_ref, 3)).astype(BF16)
    acc_ref[...] = jnp.zeros_like(acc_ref)

    def stage(slot, c):
        xm = xm_ref[...]
        for s in range(2):
            z_ref[slot, s] = jnp.dot(xm, wup_ref[s, c], preferred_element_type=F32)

    def consume(slot, c):
        def conv_half(s):
            cw = cw_ref[s, c]
            taps = [z_ref[slot, s, HALO - 1 + j:HALO - 1 + j + tm, :] * cw[j:j + 1, :] for j in range(CONV_W)]
            return taps[0] + taps[1] + taps[2] + cb_ref[s, c]

        act = (_silu(conv_half(0)) * conv_half(1)).astype(BF16)
        acc_ref[...] += jnp.dot(act, wdn_ref[c], preferred_element_type=F32)

    def body(jj, carry):
        c = FFN_PIPE_BODY * jj
        for t in range(FFN_PIPE_BODY):
            stage((t + 1) % 2, c + t + 1)
            consume(t % 2, c + t)
        return carry

    n_body = (n_chunks - 1) // FFN_PIPE_BODY
    stage(0, 0)
    lax.fori_loop(0, n_body, body, 0)
    slot = 0
    for c in range(FFN_PIPE_BODY * n_body, n_chunks - 1):
        stage(1 - slot, c + 1)
        consume(slot, c)
        slot = 1 - slot
    consume(slot, n_chunks - 1)
    y = out_ref[...] + mod_ref[5:6, :] * acc_ref[...]
    if final:
        y = _rms(y) * fg_ref[...]
    out_ref[...] = y


def _ffn(h, mod, ctx_row, w_up, conv_w, conv_b, w_down, final_g, tm, mixer=None):
    b, l, d = h.shape
    n_chunks = w_down.shape[0]
    nh = l // HALO
    per = tm // HALO
    final = final_g is not None
    kern = functools.partial(_ffn_kernel, tm=tm, n_chunks=n_chunks, final=final, mixer=mixer is not None)
    tile_specs = [pl.BlockSpec((None, tm, d), lambda b, i: (b, i, 0)),
                  pl.BlockSpec((None, HALO, d), lambda b, i: (b, jnp.maximum(i * per - 1, 0), 0)),
                  pl.BlockSpec((None, HALO, d), lambda b, i: (b, jnp.minimum((i + 1) * per, nh - 1), 0))]
    in_specs = list(tile_specs)
    args = [h, h, h]
    if mixer is not None:
        o, w_proj = mixer
        in_specs += tile_specs + [_resident(w_proj.shape)]
        args += [o, o, o, w_proj]
    in_specs += [_mod_spec(d, ctx_row),
                 _resident(w_up.shape), _resident(conv_w.shape), _resident(conv_b.shape), _resident(w_down.shape)]
    args += [mod, w_up, conv_w, conv_b, w_down]
    if final:
        in_specs.append(_resident(final_g.shape))
        args.append(final_g)
    return pl.pallas_call(
        kern,
        grid=(b, l // tm),
        in_specs=in_specs,
        out_specs=pl.BlockSpec((None, tm, d), lambda b, i: (b, i, 0)),
        out_shape=jax.ShapeDtypeStruct(h.shape, F32),
        scratch_shapes=[pltpu.VMEM((tm + 2 * HALO, d), BF16), pltpu.VMEM((tm, d), F32),
                        pltpu.VMEM((2, 2, tm + 2 * HALO, w_up.shape[3]), F32)],
        compiler_params=_params(2),
        name="ffn",
    )(*args)


def _qkv_kernel(*refs, heads, with_q):
    if with_q:
        h_ref, mod_ref, w_ref, cos_ref, sa_ref, sb_ref, qt_ref, k_ref, vt_ref = refs
    else:
        h_ref, mod_ref, w_ref, k_ref, vt_ref = refs
    xl = _modulate(h_ref[...], mod_ref, 0).astype(BF16)

    def rope(yb):
        return (yb * cos_ref[...] + pltpu.roll(yb, LANES - SUB_HEAD // 4, 1) * sa_ref[...]
                + pltpu.roll(yb, SUB_HEAD // 4, 1) * sb_ref[...])

    group = 4
    for part in range(3 if with_q else 2):
        kind = part if with_q else part + 1
        for h0 in range(0, heads, group):
            c0 = (part * heads + h0) * HEAD_W
            y = jnp.dot(xl, w_ref[:, c0:c0 + group * HEAD_W], preferred_element_type=F32)
            for hh in range(group):
                h = h0 + hh
                yb = y[:, hh * HEAD_W:(hh + 1) * HEAD_W]
                if kind == 0:
                    qt_ref[h] = (rope(yb) * (SUB_HEAD ** -0.5 * math.log2(math.e))).T.astype(BF16)
                elif kind == 1:
                    k_ref[h] = (rope(yb) if with_q else yb).astype(BF16)
                else:
                    vt_ref[h, 0:HEAD_W, :] = yb.T.astype(BF16)
                    vt_ref[h, HEAD_W:VT_ROWS, :] = jnp.ones((VT_ROWS - HEAD_W, yb.shape[0]), BF16)


def _qkv(h, mod, ctx_row, w, tables, tm):
    b, l, d = h.shape
    with_q = tables is not None
    heads = w.shape[1] // HEAD_W // (3 if with_q else 2)
    kern = functools.partial(_qkv_kernel, heads=heads, with_q=with_q)
    in_specs = [pl.BlockSpec((None, tm, d), lambda b, i: (b, i, 0)), _mod_spec(d, ctx_row), _resident(w.shape)]
    args = [h, mod, w]
    out_specs = [pl.BlockSpec((None, heads, tm, HEAD_W), lambda b, i: (b, 0, i, 0)),
                 pl.BlockSpec((None, heads, None, VT_ROWS, tm), lambda b, i: (b, 0, i, 0, 0))]
    out_shape = [jax.ShapeDtypeStruct((b, heads, l, HEAD_W), BF16),
                 jax.ShapeDtypeStruct((b, heads, l // tm, VT_ROWS, tm), BF16)]
    if with_q:
        in_specs += [pl.BlockSpec((tm, LANES), lambda b, i: (i, 0))] * 3
        args += list(tables)
        out_specs.insert(0, pl.BlockSpec((None, heads, HEAD_W, tm), lambda b, i: (b, 0, 0, i)))
        out_shape.insert(0, jax.ShapeDtypeStruct((b, heads, HEAD_W, l), BF16))
    outs = pl.pallas_call(
        kern,
        grid=(b, l // tm),
        in_specs=in_specs,
        out_specs=out_specs,
        out_shape=out_shape,
        compiler_params=_params(2),
        name="qkv",
    )(*args)
    return outs if with_q else (None, *outs)


def _rope_tables(n_lat):
    rows = n_lat // GRID_W
    row_pos = jnp.broadcast_to(jnp.arange(rows, dtype=F32)[:, None], (rows, GRID_W)).reshape(-1)
    col_pos = jnp.broadcast_to(jnp.arange(GRID_W, dtype=F32)[None, :], (rows, GRID_W)).reshape(-1)
    n_freq = SUB_HEAD // 4
    inv_freq = ROPE_THETA ** (-jnp.arange(n_freq, dtype=F32) / n_freq)
    ang_r = row_pos[:, None] * inv_freq
    ang_c = col_pos[:, None] * inv_freq
    ang = jnp.concatenate([ang_r, ang_r, ang_c, ang_c] * 2, axis=-1)
    first = (jnp.arange(LANES) % (2 * n_freq)) < n_freq
    sin = jnp.sin(ang)
    return jnp.cos(ang), jnp.where(first, -sin, 0.0), jnp.where(first, 0.0, sin)


def _attn_kernel(qt_ref, kc_ref, vtc_ref, kl_ref, vtl_ref, lam_ref, g_ref, o_ref,
                 qs_ref, m_ref, acc_ref, grow_ref, *, tq, lambda_init):
    dim = lax.broadcasted_iota(jnp.int32, (HEAD_W, tq), 0)
    qt = qt_ref[...]
    zero = jnp.zeros_like(qt)
    qs_ref[:, 0:tq] = jnp.where(dim < SUB_HEAD, qt, zero)
    qs_ref[:, tq:2 * tq] = jnp.where(dim >= SUB_HEAD, qt, zero)
    n_chunks, _, tk = vtl_ref.shape

    def lat_k(j):
        return kl_ref[pl.ds(pl.multiple_of(j * tk, tk), tk), :]

    def finish():
        lp = lam_ref[...]
        lam = (jnp.exp(jnp.sum(lp[0:1, :] * lp[1:2, :], axis=1, keepdims=True))
               - jnp.exp(jnp.sum(lp[2:3, :] * lp[3:4, :], axis=1, keepdims=True)) + lambda_init)
        ot = acc_ref[0:HEAD_W, :] / acc_ref[HEAD_W:HEAD_W + 1, :]
        od = ot[:, 0:tq] - lam * ot[:, tq:2 * tq]
        od = od * lax.rsqrt(jnp.mean(od * od, axis=0, keepdims=True) + EPS) * (g_ref[...] * (1.0 - lambda_init))
        o_ref[...] = od.T.astype(BF16)

    def single_pass_block(j):
        st = jnp.dot(lat_k(j), qs_ref[...], preferred_element_type=F32)
        m_prev = m_ref[...]
        pt = jnp.exp2(st - m_prev).astype(BF16)
        mx = jnp.max(st, axis=0, keepdims=True)
        m_new = jnp.maximum(m_prev, mx)
        grow_ref[...] = jnp.maximum(grow_ref[...], mx - m_prev)
        pv = jnp.dot(vtl_ref[j], pt, preferred_element_type=F32)
        acc_ref[...] = (acc_ref[...] + pv) * jnp.exp2(m_prev - m_new)
        m_ref[...] = m_new

    st = jnp.dot(kc_ref[...], qs_ref[...], preferred_element_type=F32)
    m_ref[...] = jnp.max(st, axis=0, keepdims=True)
    acc_ref[...] = jnp.dot(vtc_ref[0], jnp.exp2(st - m_ref[...]).astype(BF16), preferred_element_type=F32)
    grow_ref[...] = jnp.zeros_like(grow_ref)

    def single_pass_body(jj, carry):
        for t in range(PIPE_BODY):
            single_pass_block(PIPE_BODY * jj + t)
        return carry

    n_body = n_chunks // PIPE_BODY
    lax.fori_loop(0, n_body, single_pass_body, 0)
    for j in range(PIPE_BODY * n_body, n_chunks):
        single_pass_block(j)
    finish()

    @pl.when(jnp.max(grow_ref[...]) > MAX_GROWTH_LOG2)
    def _():
        m_ref[...] = jnp.full_like(m_ref, -jnp.inf)
        acc_ref[...] = jnp.zeros_like(acc_ref)

        def two_phase_block(k, vt):
            st = jnp.dot(k, qs_ref[...], preferred_element_type=F32)
            m_prev = m_ref[...]
            m_new = jnp.maximum(m_prev, jnp.max(st, axis=0, keepdims=True))
            pt = jnp.exp2(st - m_new).astype(BF16)
            acc_ref[...] = (jnp.exp2(m_prev - m_new) * acc_ref[...]
                            + jnp.dot(vt, pt, preferred_element_type=F32))
            m_ref[...] = m_new

        two_phase_block(kc_ref[...], vtc_ref[0])

        def two_phase_body(j, carry):
            two_phase_block(lat_k(j), vtl_ref[j])
            return carry

        lax.fori_loop(0, n_chunks, two_phase_body, 0)
        finish()


def _attn(qt, k_ctx, vt_ctx, k_lat, vt_lat, lam_params, subln_g, lambda_init, tq):
    b, heads, _, n_lat = qt.shape
    n_ctx = k_ctx.shape[2]
    kern = functools.partial(_attn_kernel, tq=tq, lambda_init=lambda_init)
    vt_block = lambda a: pl.BlockSpec((None, None) + a.shape[2:], lambda b, h, i: (b, h, 0, 0, 0))
    return pl.pallas_call(
        kern,
        grid=(b, heads, n_lat // tq),
        in_specs=[pl.BlockSpec((None, None, HEAD_W, tq), lambda b, h, i: (b, h, 0, i)),
                  pl.BlockSpec((None, None, n_ctx, HEAD_W), lambda b, h, i: (b, h, 0, 0)),
                  vt_block(vt_ctx),
                  pl.BlockSpec((None, None, n_lat, HEAD_W), lambda b, h, i: (b, h, 0, 0)),
                  vt_block(vt_lat),
                  pl.BlockSpec(lam_params.shape, lambda b, h, i: (0, 0)),
                  pl.BlockSpec(subln_g.shape, lambda b, h, i: (0, 0))],
        out_specs=pl.BlockSpec((None, tq, HEAD_W), lambda b, h, i: (b, i, h)),
        out_shape=jax.ShapeDtypeStruct((b, n_lat, heads * HEAD_W), BF16),
        scratch_shapes=[pltpu.VMEM((HEAD_W, 2 * tq), BF16), pltpu.VMEM((1, 2 * tq), F32),
                        pltpu.VMEM((VT_ROWS, 2 * tq), F32), pltpu.VMEM((1, 2 * tq), F32)],
        compiler_params=_params(3),
        name="attn",
    )(qt, k_ctx, vt_ctx, k_lat, vt_lat, lam_params, subln_g)


def _tile(l, target):
    return min(l, target)


def kernel(x, c, ctx, c_ctx, ada_w, ada_b, gm_w_in, gm_norm_g, gm_w_s, gm_b_s, gm_w_out, da_w_qkv, da_lambda_q1, da_lambda_k1, da_lambda_q2, da_lambda_k2, da_subln_g, da_w_out, ffn_w_up, ffn_conv_w, ffn_conv_b, ffn_w_down, final_norm_g):
    bsz, n_lat, d = x.shape
    depth = ada_w.shape[0]
    ffn_dim = ffn_w_down.shape[1]
    assert depth == 2 and bsz < ADA_ROWS and n_lat % GRID_W == 0
    ctx_row = bsz
    ffn_cw = 2 * LANES
    n_chunks = ffn_dim // ffn_cw
    assert n_chunks * ffn_cw == ffn_dim

    cond = jnp.zeros((ADA_ROWS, d), F32).at[:bsz].set(c).at[ctx_row].set(c_ctx)
    mod = _ada(cond, ada_w, ada_b).reshape(depth, ADA_ROWS, 6, d)

    def ffn_weights(i):
        w_up = ffn_w_up[i].astype(BF16).reshape(d, 2, n_chunks, ffn_cw).transpose(1, 2, 0, 3)
        cw = ffn_conv_w[i].reshape(CONV_W, 2, n_chunks, ffn_cw).transpose(1, 2, 0, 3)
        cb = ffn_conv_b[i].reshape(2, n_chunks, 1, ffn_cw)
        w_dn = ffn_w_down[i].astype(BF16).reshape(n_chunks, ffn_cw, d)
        return w_up, cw, cb, w_dn

    tm_lat = _tile(n_lat, TOKEN_TILE)
    tm_ctx = _tile(ctx.shape[1], TOKEN_TILE)

    gm = (gm_w_in[0].astype(BF16), gm_norm_g[0].reshape(1, -1), gm_w_s[0].astype(BF16),
          gm_b_s[0][:, :, None], gm_w_out[0].astype(BF16))
    fw = ffn_weights(0)
    h = _gmlp(x, mod[0], None, *gm, tm_lat)
    hc = _gmlp(ctx, mod[0], ctx_row, *gm, tm_ctx)
    h = _ffn(h, mod[0], None, *fw, None, tm_lat)
    hc = _ffn(hc, mod[0], ctx_row, *fw, None, tm_ctx)

    lambda_init = 0.8 - 0.6 * math.exp(-0.3 * 1)
    w_qkv = da_w_qkv[0].astype(BF16)
    da_width = w_qkv.shape[1] // 3
    qt, k_lat, vt_lat = _qkv(h, mod[1], None, w_qkv, _rope_tables(n_lat), tm_lat)
    _, k_ctx, vt_ctx = _qkv(hc, mod[1], ctx_row, w_qkv[:, da_width:], None, tm_ctx)
    lam_params = jnp.stack([da_lambda_q1[0], da_lambda_k1[0], da_lambda_q2[0], da_lambda_k2[0]])
    o = _attn(qt, k_ctx, vt_ctx, k_lat, vt_lat, lam_params, da_subln_g[0].reshape(-1, 1), lambda_init,
              _tile(n_lat, QUERY_TILE))
    fw = ffn_weights(1)
    return _ffn(h, mod[1], None, *fw, final_norm_g.reshape(1, -1), tm_lat, mixer=(o, da_w_out[0].astype(BF16)))
```

```python
import functools
import math

import jax
import jax.numpy as jnp
from jax import lax
from jax.experimental import pallas as pl
from jax.experimental.pallas import tpu as pltpu

EPS = 1e-6
GRID_W = 64
GM_GROUPS = 8
CHUNK = 128
SUB_HEAD = 64
HEAD_W = 2 * SUB_HEAD
VT_ROWS = HEAD_W + 16
ROPE_THETA = 10000.0
CONV_W = 3
HALO = 16
LANES = 128
SUBLANES = 8
ADA_ROWS = 16
PIPE_BODY = 16
TOKEN_TILE = 1024
QUERY_TILE = 1024
MAX_GROWTH_LOG2 = 32.0
FFN_PIPE_BODY = 2
VMEM_LIMIT_BYTES = 60 * 1024 * 1024

F32 = jnp.float32
BF16 = jnp.bfloat16


def _rms(x):
    return x * lax.rsqrt(jnp.mean(x * x, axis=-1, keepdims=True) + EPS)


def _modulate(x, mod_ref, k):
    return _rms(x) * (1.0 + mod_ref[k + 1:k + 2, :]) + mod_ref[k:k + 1, :]


def _silu(x):
    return x * (1.0 / (1.0 + jnp.exp(-x)))


def _gelu_tanh(x):
    return 0.5 * x * (1.0 + jnp.tanh(math.sqrt(2.0 / math.pi) * (x + 0.044715 * (x * x * x))))


def _params(n_axes):
    return pltpu.CompilerParams(dimension_semantics=("arbitrary",) * n_axes,
                                vmem_limit_bytes=VMEM_LIMIT_BYTES)


def _resident(shape):
    nd = len(shape)
    return pl.BlockSpec(shape, lambda *_: (0,) * nd, pipeline_mode=pl.Buffered(1))


def _ada_kernel(cond_ref, w_ref, b_ref, out_ref):
    s = _silu(cond_ref[...])
    out_ref[...] = jnp.dot(s, w_ref[...], preferred_element_type=F32,
                           precision=lax.Precision.HIGHEST) + b_ref[...]


def _ada(cond, ada_w, ada_b):
    depth, d, n = ada_w.shape
    nb = 6 * LANES * 2
    return pl.pallas_call(
        _ada_kernel,
        grid=(depth, n // nb),
        in_specs=[pl.BlockSpec((ADA_ROWS, d), lambda l, j: (0, 0)),
                  pl.BlockSpec((None, d, nb), lambda l, j: (l, 0, j)),
                  pl.BlockSpec((None, 1, nb), lambda l, j: (l, 0, j))],
        out_specs=pl.BlockSpec((None, ADA_ROWS, nb), lambda l, j: (l, 0, j)),
        out_shape=jax.ShapeDtypeStruct((depth, ADA_ROWS, n), F32),
        compiler_params=_params(2),
        name="ada",
    )(cond, ada_w, ada_b.reshape(depth, 1, n))


def _mod_spec(d, ctx_row):
    if ctx_row is None:
        return pl.BlockSpec((None, 6, d), lambda b, i: (b, 0, 0))
    return pl.BlockSpec((None, 6, d), lambda b, i: (ctx_row, 0, 0))


def _gmlp_kernel(h_ref, mod_ref, win_ref, ng_ref, ws_ref, bs_ref, wout_ref, out_ref, vn_ref, uv_ref, *, tm, gw):
    x = h_ref[...]
    xl = _modulate(x, mod_ref, 0).astype(BF16)
    v = _gelu_tanh(jnp.dot(xl, win_ref[:, gw:], preferred_element_type=F32))
    vn_ref[...] = (_rms(v) * ng_ref[...]).astype(BF16)
    gd = gw // GM_GROUPS
    for g in range(GM_GROUPS):
        cols = slice(g * gd, (g + 1) * gd)
        u = _gelu_tanh(jnp.dot(xl, win_ref[:, cols], preferred_element_type=F32))
        for c in range(tm // CHUNK):
            rows = slice(c * CHUNK, (c + 1) * CHUNK)
            mix = jnp.dot(ws_ref[g], vn_ref[rows, cols], preferred_element_type=F32) + bs_ref[g]
            uv_ref[rows, cols] = (u[rows, :] * mix).astype(BF16)
    out = jnp.dot(uv_ref[...], wout_ref[...], preferred_element_type=F32)
    out_ref[...] = x + mod_ref[2:3, :] * out


def _gmlp(h, mod, ctx_row, w_in, norm_g, w_s, b_s, w_out, tm):
    b, l, d = h.shape
    gw = w_out.shape[0]
    kern = functools.partial(_gmlp_kernel, tm=tm, gw=gw)
    return pl.pallas_call(
        kern,
        grid=(b, l // tm),
        in_specs=[pl.BlockSpec((None, tm, d), lambda b, i: (b, i, 0)),
                  _mod_spec(d, ctx_row),
                  _resident(w_in.shape), _resident(norm_g.shape), _resident(w_s.shape),
                  _resident(b_s.shape), _resident(w_out.shape)],
        out_specs=pl.BlockSpec((None, tm, d), lambda b, i: (b, i, 0)),
        out_shape=jax.ShapeDtypeStruct(h.shape, F32),
        scratch_shapes=[pltpu.VMEM((tm, gw), BF16), pltpu.VMEM((tm, gw), BF16)],
        compiler_params=_params(2),
        name="gmlp",
    )(h, mod, w_in, norm_g, w_s, b_s, w_out)


def _ffn_kernel(*refs, tm, n_chunks, final, mixer):
    refs = list(refs)
    h_refs = refs[0:3]
    del refs[0:3]
    if mixer:
        o_refs, wproj_ref = refs[0:3], refs[3]
        del refs[0:4]
    mod_ref, wup_ref, cw_ref, cb_ref, wdn_ref = refs[0:5]
    del refs[0:5]
    if final:
        fg_ref = refs.pop(0)
    out_ref, xm_ref, acc_ref, z_ref, act_ref = refs
    i = pl.program_id(1)
    last = pl.num_programs(1) - 1

    def stream(k):
        if not mixer:
            return h_refs[k][...]
        return h_refs[k][...] + mod_ref[2:3, :] * jnp.dot(o_refs[k][...], wproj_ref[...], preferred_element_type=F32)

    out_ref[...] = stream(0)
    xm_ref[0:HALO, :] = jnp.where(i == 0, 0.0, _modulate(stream(1), mod_ref, 3)).astype(BF16)
    xm_ref[HALO:HALO + tm, :] = _modulate(out_ref[...], mod_ref, 3).astype(BF16)
    xm_ref[HALO + tm:, :] = jnp.where(i == last, 0.0, _modulate(stream(2), mod_ref, 3)).astype(BF16)
    acc_ref[...] = jnp.zeros_like(acc_ref)

    def stage(slot, c):
        xm = xm_ref[...]
        for s in range(2):
            z_ref[slot, s] = jnp.dot(xm, wup_ref[s, c], preferred_element_type=F32)

    sub = lax.broadcasted_iota(jnp.int32, (1, SUBLANES, 1), 1)
    g0, ng = HALO // SUBLANES, tm // SUBLANES

    def consume(slot, c, role):
        def conv_half(s, scale):
            cw = cw_ref[s, c] * scale
            z3 = z_ref[slot, s].reshape(-1, SUBLANES, cw.shape[-1])
            rd = pltpu.roll(z3, 1, 1)
            ru = pltpu.roll(z3, SUBLANES - 1, 1)
            zd = jnp.where(sub == 0, rd[g0 - 1:g0 - 1 + ng], rd[g0:g0 + ng])
            zu = jnp.where(sub == SUBLANES - 1, ru[g0 + 1:g0 + 1 + ng], ru[g0:g0 + ng])
            zc = zd * cw[0:1, :] + z3[g0:g0 + ng] * cw[1:2, :] + zu * cw[2:3, :] + cb_ref[s, c] * scale
            return zc.reshape(tm, cw.shape[-1])

        half_a = conv_half(0, 0.5)
        act = ((half_a * jnp.tanh(half_a) + half_a) * conv_half(1, 1.0)).astype(BF16)
        if role == "first":
            act_ref[...] = act
        elif role == "second":
            acc_ref[...] += (jnp.dot(act_ref[...], wdn_ref[c - 1], preferred_element_type=F32)
                             + jnp.dot(act, wdn_ref[c], preferred_element_type=F32))
        else:
            acc_ref[...] += jnp.dot(act, wdn_ref[c], preferred_element_type=F32)

    def body(jj, carry):
        c = FFN_PIPE_BODY * jj
        for t in range(FFN_PIPE_BODY):
            stage((t + 1) % 2, c + t + 1)
            consume(t % 2, c + t, ("first", "second")[t % 2])
        return carry

    n_body = (n_chunks - 1) // FFN_PIPE_BODY
    stage(0, 0)
    lax.fori_loop(0, n_body, body, 0)
    slot = 0
    for c in range(FFN_PIPE_BODY * n_body, n_chunks - 1):
        stage(1 - slot, c + 1)
        consume(slot, c, "single")
        slot = 1 - slot
    consume(slot, n_chunks - 1, "single")
    y = out_ref[...] + mod_ref[5:6, :] * acc_ref[...]
    if final:
        y = _rms(y) * fg_ref[...]
    out_ref[...] = y


def _ffn(h, mod, ctx_row, w_up, conv_w, conv_b, w_down, final_g, tm, mixer=None):
    b, l, d = h.shape
    n_chunks = w_down.shape[0]
    nh = l // HALO
    per = tm // HALO
    final = final_g is not None
    kern = functools.partial(_ffn_kernel, tm=tm, n_chunks=n_chunks, final=final, mixer=mixer is not None)
    tile_specs = [pl.BlockSpec((None, tm, d), lambda b, i: (b, i, 0)),
                  pl.BlockSpec((None, HALO, d), lambda b, i: (b, jnp.maximum(i * per - 1, 0), 0)),
                  pl.BlockSpec((None, HALO, d), lambda b, i: (b, jnp.minimum((i + 1) * per, nh - 1), 0))]
    in_specs = list(tile_specs)
    args = [h, h, h]
    if mixer is not None:
        o, w_proj = mixer
        in_specs += tile_specs + [_resident(w_proj.shape)]
        args += [o, o, o, w_proj]
    in_specs += [_mod_spec(d, ctx_row),
                 _resident(w_up.shape), _resident(conv_w.shape), _resident(conv_b.shape), _resident(w_down.shape)]
    args += [mod, w_up, conv_w, conv_b, w_down]
    if final:
        in_specs.append(_resident(final_g.shape))
        args.append(final_g)
    return pl.pallas_call(
        kern,
        grid=(b, l // tm),
        in_specs=in_specs,
        out_specs=pl.BlockSpec((None, tm, d), lambda b, i: (b, i, 0)),
        out_shape=jax.ShapeDtypeStruct(h.shape, F32),
        scratch_shapes=[pltpu.VMEM((tm + 2 * HALO, d), BF16), pltpu.VMEM((tm, d), F32),
                        pltpu.VMEM((2, 2, tm + 2 * HALO, w_up.shape[3]), F32),
                        pltpu.VMEM((tm, w_up.shape[3]), BF16)],
        compiler_params=_params(2),
        name="ffn",
    )(*args)


def _qkv_kernel(*refs, heads, with_q):
    if with_q:
        h_ref, mod_ref, w_ref, cos_ref, sa_ref, sb_ref, qt_ref, k_ref, vt_ref = refs
    else:
        h_ref, mod_ref, w_ref, k_ref, vt_ref = refs
    xl = _modulate(h_ref[...], mod_ref, 0).astype(BF16)

    def rope(yb):
        return (yb * cos_ref[...] + pltpu.roll(yb, LANES - SUB_HEAD // 4, 1) * sa_ref[...]
                + pltpu.roll(yb, SUB_HEAD // 4, 1) * sb_ref[...])

    group = 4
    for part in range(3 if with_q else 2):
        kind = part if with_q else part + 1
        for h0 in range(0, heads, group):
            c0 = (part * heads + h0) * HEAD_W
            y = jnp.dot(xl, w_ref[:, c0:c0 + group * HEAD_W], preferred_element_type=F32)
            for hh in range(group):
                h = h0 + hh
                yb = y[:, hh * HEAD_W:(hh + 1) * HEAD_W]
                if kind == 0:
                    qt_ref[h] = (rope(yb) * (SUB_HEAD ** -0.5 * math.log2(math.e))).T.astype(BF16)
                elif kind == 1:
                    k_ref[h] = (rope(yb) if with_q else yb).astype(BF16)
                else:
                    vt_ref[h, 0:HEAD_W, :] = yb.T.astype(BF16)
                    vt_ref[h, HEAD_W:VT_ROWS, :] = jnp.ones((VT_ROWS - HEAD_W, yb.shape[0]), BF16)


def _qkv(h, mod, ctx_row, w, tables, tm):
    b, l, d = h.shape
    with_q = tables is not None
    heads = w.shape[1] // HEAD_W // (3 if with_q else 2)
    kern = functools.partial(_qkv_kernel, heads=heads, with_q=with_q)
    in_specs = [pl.BlockSpec((None, tm, d), lambda b, i: (b, i, 0)), _mod_spec(d, ctx_row), _resident(w.shape)]
    args = [h, mod, w]
    out_specs = [pl.BlockSpec((None, heads, tm, HEAD_W), lambda b, i: (b, 0, i, 0)),
                 pl.BlockSpec((None, heads, None, VT_ROWS, tm), lambda b, i: (b, 0, i, 0, 0))]
    out_shape = [jax.ShapeDtypeStruct((b, heads, l, HEAD_W), BF16),
                 jax.ShapeDtypeStruct((b, heads, l // tm, VT_ROWS, tm), BF16)]
    if with_q:
        in_specs += [pl.BlockSpec((tm, LANES), lambda b, i: (i, 0))] * 3
        args += list(tables)
        out_specs.insert(0, pl.BlockSpec((None, heads, HEAD_W, tm), lambda b, i: (b, 0, 0, i)))
        out_shape.insert(0, jax.ShapeDtypeStruct((b, heads, HEAD_W, l), BF16))
    outs = pl.pallas_call(
        kern,
        grid=(b, l // tm),
        in_specs=in_specs,
        out_specs=out_specs,
        out_shape=out_shape,
        compiler_params=_params(2),
        name="qkv",
    )(*args)
    return outs if with_q else (None, *outs)


def _rope_tables(n_lat):
    rows = n_lat // GRID_W
    row_pos = jnp.broadcast_to(jnp.arange(rows, dtype=F32)[:, None], (rows, GRID_W)).reshape(-1)
    col_pos = jnp.broadcast_to(jnp.arange(GRID_W, dtype=F32)[None, :], (rows, GRID_W)).reshape(-1)
    n_freq = SUB_HEAD // 4
    inv_freq = ROPE_THETA ** (-jnp.arange(n_freq, dtype=F32) / n_freq)
    ang_r = row_pos[:, None] * inv_freq
    ang_c = col_pos[:, None] * inv_freq
    ang = jnp.concatenate([ang_r, ang_r, ang_c, ang_c] * 2, axis=-1)
    first = (jnp.arange(LANES) % (2 * n_freq)) < n_freq
    sin = jnp.sin(ang)
    return jnp.cos(ang), jnp.where(first, -sin, 0.0), jnp.where(first, 0.0, sin)


def _attn_kernel(qt_ref, kc_ref, vtc_ref, kl_ref, vtl_ref, lam_ref, g_ref, o_ref,
                 qs_ref, m_ref, acc_ref, grow_ref, *, tq, lambda_init):
    dim = lax.broadcasted_iota(jnp.int32, (HEAD_W, tq), 0)
    qt = qt_ref[...]
    zero = jnp.zeros_like(qt)
    qs_ref[:, 0:tq] = jnp.where(dim < SUB_HEAD, qt, zero)
    qs_ref[:, tq:2 * tq] = jnp.where(dim >= SUB_HEAD, qt, zero)
    n_chunks, _, tk = vtl_ref.shape

    def lat_k(j):
        return kl_ref[pl.ds(pl.multiple_of(j * tk, tk), tk), :]

    def finish():
        lp = lam_ref[...]
        lam = (jnp.exp(jnp.sum(lp[0:1, :] * lp[1:2, :], axis=1, keepdims=True))
               - jnp.exp(jnp.sum(lp[2:3, :] * lp[3:4, :], axis=1, keepdims=True)) + lambda_init)
        ot = acc_ref[0:HEAD_W, :] / acc_ref[HEAD_W:HEAD_W + 1, :]
        od = ot[:, 0:tq] - lam * ot[:, tq:2 * tq]
        od = od * lax.rsqrt(jnp.mean(od * od, axis=0, keepdims=True) + EPS) * (g_ref[...] * (1.0 - lambda_init))
        o_ref[...] = od.T.astype(BF16)

    def single_pass_block(j):
        st = jnp.dot(lat_k(j), qs_ref[...], preferred_element_type=F32)
        m_prev = m_ref[...]
        pt = jnp.exp2(st - m_prev).astype(BF16)
        mx = jnp.max(st, axis=0, keepdims=True)
        m_new = jnp.maximum(m_prev, mx)
        grow_ref[...] = jnp.maximum(grow_ref[...], mx - m_prev)
        pv = jnp.dot(vtl_ref[j], pt, preferred_element_type=F32)
        acc_ref[...] = (acc_ref[...] + pv) * jnp.exp2(m_prev - m_new)
        m_ref[...] = m_new

    st = jnp.dot(kc_ref[...], qs_ref[...], preferred_element_type=F32)
    m_ref[...] = jnp.max(st, axis=0, keepdims=True)
    acc_ref[...] = jnp.dot(vtc_ref[0], jnp.exp2(st - m_ref[...]).astype(BF16), preferred_element_type=F32)
    grow_ref[...] = jnp.zeros_like(grow_ref)

    def single_pass_body(jj, carry):
        for t in range(PIPE_BODY):
            single_pass_block(PIPE_BODY * jj + t)
        return carry

    n_body = n_chunks // PIPE_BODY
    lax.fori_loop(0, n_body, single_pass_body, 0)
    for j in range(PIPE_BODY * n_body, n_chunks):
        single_pass_block(j)
    finish()

    @pl.when(jnp.max(grow_ref[...]) > MAX_GROWTH_LOG2)
    def _():
        m_ref[...] = jnp.full_like(m_ref, -jnp.inf)
        acc_ref[...] = jnp.zeros_like(acc_ref)

        def two_phase_block(k, vt):
            st = jnp.dot(k, qs_ref[...], preferred_element_type=F32)
            m_prev = m_ref[...]
            m_new = jnp.maximum(m_prev, jnp.max(st, axis=0, keepdims=True))
            pt = jnp.exp2(st - m_new).astype(BF16)
            acc_ref[...] = (jnp.exp2(m_prev - m_new) * acc_ref[...]
                            + jnp.dot(vt, pt, preferred_element_type=F32))
            m_ref[...] = m_new

        two_phase_block(kc_ref[...], vtc_ref[0])

        def two_phase_body(j, carry):
            two_phase_block(lat_k(j), vtl_ref[j])
            return carry

        lax.fori_loop(0, n_chunks, two_phase_body, 0)
        finish()


def _attn(qt, k_ctx, vt_ctx, k_lat, vt_lat, lam_params, subln_g, lambda_init, tq):
    b, heads, _, n_lat = qt.shape
    n_ctx = k_ctx.shape[2]
    kern = functools.partial(_attn_kernel, tq=tq, lambda_init=lambda_init)
    vt_block = lambda a: pl.BlockSpec((None, None) + a.shape[2:], lambda b, h, i: (b, h, 0, 0, 0))
    return pl.pallas_call(
        kern,
        grid=(b, heads, n_lat // tq),
        in_specs=[pl.BlockSpec((None, None, HEAD_W, tq), lambda b, h, i: (b, h, 0, i)),
                  pl.BlockSpec((None, None, n_ctx, HEAD_W), lambda b, h, i: (b, h, 0, 0)),
                  vt_block(vt_ctx),
                  pl.BlockSpec((None, None, n_lat, HEAD_W), lambda b, h, i: (b, h, 0, 0)),
                  vt_block(vt_lat),
                  pl.BlockSpec(lam_params.shape, lambda b, h, i: (0, 0)),
                  pl.BlockSpec(subln_g.shape, lambda b, h, i: (0, 0))],
        out_specs=pl.BlockSpec((None, tq, HEAD_W), lambda b, h, i: (b, i, h)),
        out_shape=jax.ShapeDtypeStruct((b, n_lat, heads * HEAD_W), BF16),
        scratch_shapes=[pltpu.VMEM((HEAD_W, 2 * tq), BF16), pltpu.VMEM((1, 2 * tq), F32),
                        pltpu.VMEM((VT_ROWS, 2 * tq), F32), pltpu.VMEM((1, 2 * tq), F32)],
        compiler_params=_params(3),
        name="attn",
    )(qt, k_ctx, vt_ctx, k_lat, vt_lat, lam_params, subln_g)


def _tile(l, target):
    return min(l, target)


def kernel(x, c, ctx, c_ctx, ada_w, ada_b, gm_w_in, gm_norm_g, gm_w_s, gm_b_s, gm_w_out, da_w_qkv, da_lambda_q1, da_lambda_k1, da_lambda_q2, da_lambda_k2, da_subln_g, da_w_out, ffn_w_up, ffn_conv_w, ffn_conv_b, ffn_w_down, final_norm_g):
    bsz, n_lat, d = x.shape
    depth = ada_w.shape[0]
    ffn_dim = ffn_w_down.shape[1]
    assert depth == 2 and bsz < ADA_ROWS and n_lat % GRID_W == 0
    ctx_row = bsz
    ffn_cw = 2 * LANES
    n_chunks = ffn_dim // ffn_cw
    assert n_chunks * ffn_cw == ffn_dim

    cond = jnp.zeros((ADA_ROWS, d), F32).at[:bsz].set(c).at[ctx_row].set(c_ctx)
    mod = _ada(cond, ada_w, ada_b).reshape(depth, ADA_ROWS, 6, d)

    def ffn_weights(i):
        w_up = ffn_w_up[i].astype(BF16).reshape(d, 2, n_chunks, ffn_cw).transpose(1, 2, 0, 3)
        cw = ffn_conv_w[i].reshape(CONV_W, 2, n_chunks, ffn_cw).transpose(1, 2, 0, 3)
        cb = ffn_conv_b[i].reshape(2, n_chunks, 1, ffn_cw)
        w_dn = ffn_w_down[i].astype(BF16).reshape(n_chunks, ffn_cw, d)
        return w_up, cw, cb, w_dn

    tm_lat = _tile(n_lat, TOKEN_TILE)
    tm_ctx = _tile(ctx.shape[1], TOKEN_TILE)

    gm = (gm_w_in[0].astype(BF16), gm_norm_g[0].reshape(1, -1), gm_w_s[0].astype(BF16),
          gm_b_s[0][:, :, None], gm_w_out[0].astype(BF16))
    fw = ffn_weights(0)
    h = _gmlp(x, mod[0], None, *gm, tm_lat)
    hc = _gmlp(ctx, mod[0], ctx_row, *gm, tm_ctx)
    h = _ffn(h, mod[0], None, *fw, None, tm_lat)
    hc = _ffn(hc, mod[0], ctx_row, *fw, None, tm_ctx)

    lambda_init = 0.8 - 0.6 * math.exp(-0.3 * 1)
    w_qkv = da_w_qkv[0].astype(BF16)
    da_width = w_qkv.shape[1] // 3
    qt, k_lat, vt_lat = _qkv(h, mod[1], None, w_qkv, _rope_tables(n_lat), tm_lat)
    _, k_ctx, vt_ctx = _qkv(hc, mod[1], ctx_row, w_qkv[:, da_width:], None, tm_ctx)
    lam_params = jnp.stack([da_lambda_q1[0], da_lambda_k1[0], da_lambda_q2[0], da_lambda_k2[0]])
    o = _attn(qt, k_ctx, vt_ctx, k_lat, vt_lat, lam_params, da_subln_g[0].reshape(-1, 1), lambda_init,
              _tile(n_lat, QUERY_TILE))
    fw = ffn_weights(1)
    return _ffn(h, mod[1], None, *fw, final_norm_g.reshape(1, -1), tm_lat, mixer=(o, da_w_out[0].astype(BF16)))
```

```python
import functools
import math

import jax
import jax.numpy as jnp
from jax import lax
from jax.experimental import pallas as pl
from jax.experimental.pallas import tpu as pltpu

EPS = 1e-6
GRID_W = 64
GM_GROUPS = 8
CHUNK = 128
SUB_HEAD = 64
HEAD_W = 2 * SUB_HEAD
VT_ROWS = HEAD_W + 16
ROPE_THETA = 10000.0
CONV_W = 3
HALO = 16
LANES = 128
SUBLANES = 8
ADA_ROWS = 16
PIPE_BODY = 16
TOKEN_TILE = 1024
QUERY_TILE = 1024
MAX_GROWTH_LOG2 = 32.0
FFN_PIPE_BODY = 2
VMEM_LIMIT_BYTES = 60 * 1024 * 1024

F32 = jnp.float32
BF16 = jnp.bfloat16


def _rms(x):
    return x * lax.rsqrt(jnp.mean(x * x, axis=-1, keepdims=True) + EPS)


def _modulate(x, mod_ref, k):
    return _rms(x) * (1.0 + mod_ref[k + 1:k + 2, :]) + mod_ref[k:k + 1, :]


def _silu(x):
    return x * (1.0 / (1.0 + jnp.exp(-x)))


def _gelu_tanh(x):
    return 0.5 * x * (1.0 + jnp.tanh(math.sqrt(2.0 / math.pi) * (x + 0.044715 * (x * x * x))))


def _params(n_axes):
    return pltpu.CompilerParams(dimension_semantics=("arbitrary",) * n_axes,
                                vmem_limit_bytes=VMEM_LIMIT_BYTES)


def _resident(shape):
    nd = len(shape)
    return pl.BlockSpec(shape, lambda *_: (0,) * nd, pipeline_mode=pl.Buffered(1))


def _ada_kernel(cond_ref, w_ref, b_ref, out_ref):
    s = _silu(cond_ref[...])
    out_ref[...] = jnp.dot(s, w_ref[...], preferred_element_type=F32,
                           precision=lax.Precision.HIGHEST) + b_ref[...]


def _ada(cond, ada_w, ada_b):
    depth, d, n = ada_w.shape
    nb = 6 * LANES * 2
    return pl.pallas_call(
        _ada_kernel,
        grid=(depth, n // nb),
        in_specs=[pl.BlockSpec((ADA_ROWS, d), lambda l, j: (0, 0)),
                  pl.BlockSpec((None, d, nb), lambda l, j: (l, 0, j)),
                  pl.BlockSpec((None, 1, nb), lambda l, j: (l, 0, j))],
        out_specs=pl.BlockSpec((None, ADA_ROWS, nb), lambda l, j: (l, 0, j)),
        out_shape=jax.ShapeDtypeStruct((depth, ADA_ROWS, n), F32),
        compiler_params=_params(2),
        name="ada",
    )(cond, ada_w, ada_b.reshape(depth, 1, n))


def _mod_spec(d, ctx_row):
    if ctx_row is None:
        return pl.BlockSpec((None, 6, d), lambda b, i: (b, 0, 0))
    return pl.BlockSpec((None, 6, d), lambda b, i: (ctx_row, 0, 0))


def _gmlp_kernel(h_ref, mod_ref, win_ref, ng_ref, ws_ref, bs_ref, wout_ref, out_ref, vn_ref, uv_ref, *, tm, gw):
    x = h_ref[...]
    xl = _modulate(x, mod_ref, 0).astype(BF16)
    v = _gelu_tanh(jnp.dot(xl, win_ref[:, gw:], preferred_element_type=F32))
    vn_ref[...] = (_rms(v) * ng_ref[...]).astype(BF16)
    gd = gw // GM_GROUPS
    for g in range(GM_GROUPS):
        cols = slice(g * gd, (g + 1) * gd)
        u = _gelu_tanh(jnp.dot(xl, win_ref[:, cols], preferred_element_type=F32))
        for c in range(tm // CHUNK):
            rows = slice(c * CHUNK, (c + 1) * CHUNK)
            mix = jnp.dot(ws_ref[g], vn_ref[rows, cols], preferred_element_type=F32) + bs_ref[g]
            uv_ref[rows, cols] = (u[rows, :] * mix).astype(BF16)
    out = jnp.dot(uv_ref[...], wout_ref[...], preferred_element_type=F32)
    out_ref[...] = x + mod_ref[2:3, :] * out


def _gmlp(h, mod, ctx_row, w_in, norm_g, w_s, b_s, w_out, tm):
    b, l, d = h.shape
    gw = w_out.shape[0]
    kern = functools.partial(_gmlp_kernel, tm=tm, gw=gw)
    return pl.pallas_call(
        kern,
        grid=(b, l // tm),
        in_specs=[pl.BlockSpec((None, tm, d), lambda b, i: (b, i, 0)),
                  _mod_spec(d, ctx_row),
                  _resident(w_in.shape), _resident(norm_g.shape), _resident(w_s.shape),
                  _resident(b_s.shape), _resident(w_out.shape)],
        out_specs=pl.BlockSpec((None, tm, d), lambda b, i: (b, i, 0)),
        out_shape=jax.ShapeDtypeStruct(h.shape, F32),
        scratch_shapes=[pltpu.VMEM((tm, gw), BF16), pltpu.VMEM((tm, gw), BF16)],
        compiler_params=_params(2),
        name="gmlp",
    )(h, mod, w_in, norm_g, w_s, b_s, w_out)


def _ffn_kernel(*refs, tm, n_chunks, final, mixer):
    refs = list(refs)
    h_refs = refs[0:3]
    del refs[0:3]
    if mixer:
        o_refs, wproj_ref = refs[0:3], refs[3]
        del refs[0:4]
    mod_ref, wup_ref, cw_ref, cb_ref, wdn_ref = refs[0:5]
    del refs[0:5]
    if final:
        fg_ref = refs.pop(0)
    out_ref, xm_ref, acc_ref, z_ref, act_ref = refs
    i = pl.program_id(1)
    last = pl.num_programs(1) - 1

    def stream(k):
        if not mixer:
            return h_refs[k][...]
        return h_refs[k][...] + mod_ref[2:3, :] * jnp.dot(o_refs[k][...], wproj_ref[...], preferred_element_type=F32)

    out_ref[...] = stream(0)
    xm_ref[0:HALO, :] = jnp.where(i == 0, 0.0, _modulate(stream(1), mod_ref, 3)).astype(BF16)
    xm_ref[HALO:HALO + tm, :] = _modulate(out_ref[...], mod_ref, 3).astype(BF16)
    xm_ref[HALO + tm:, :] = jnp.where(i == last, 0.0, _modulate(stream(2), mod_ref, 3)).astype(BF16)
    acc_ref[...] = jnp.zeros_like(acc_ref)

    def stage(slot, c):
        xm = xm_ref[...]
        for s in range(2):
            z_ref[slot, s] = jnp.dot(xm, wup_ref[s, c], preferred_element_type=F32)

    sub = lax.broadcasted_iota(jnp.int32, (1, SUBLANES, 1), 1)
    g0, ng = HALO // SUBLANES, tm // SUBLANES

    def consume(slot, c, role):
        def conv_half(s, scale):
            cw = cw_ref[s, c] * scale
            z3 = z_ref[slot, s].reshape(-1, SUBLANES, cw.shape[-1])
            rd = pltpu.roll(z3, 1, 1)
            ru = pltpu.roll(z3, SUBLANES - 1, 1)
            zd = jnp.where(sub == 0, rd[g0 - 1:g0 - 1 + ng], rd[g0:g0 + ng])
            zu = jnp.where(sub == SUBLANES - 1, ru[g0 + 1:g0 + 1 + ng], ru[g0:g0 + ng])
            zc = zd * cw[0:1, :] + z3[g0:g0 + ng] * cw[1:2, :] + zu * cw[2:3, :] + cb_ref[s, c] * scale
            return zc.reshape(tm, cw.shape[-1])

        half_a = conv_half(0, 0.5)
        act = ((half_a * jnp.tanh(half_a) + half_a) * conv_half(1, 1.0)).astype(BF16)
        if role == "first":
            act_ref[...] = act
        elif role == "second":
            acc_ref[...] += (jnp.dot(act_ref[...], wdn_ref[c - 1], preferred_element_type=F32)
                             + jnp.dot(act, wdn_ref[c], preferred_element_type=F32))
        else:
            acc_ref[...] += jnp.dot(act, wdn_ref[c], preferred_element_type=F32)

    def body(jj, carry):
        c = FFN_PIPE_BODY * jj
        for t in range(FFN_PIPE_BODY):
            stage((t + 1) % 2, c + t + 1)
            consume(t % 2, c + t, ("first", "second")[t % 2])
        return carry

    n_body = (n_chunks - 1) // FFN_PIPE_BODY
    stage(0, 0)
    lax.fori_loop(0, n_body, body, 0)
    slot = 0
    for c in range(FFN_PIPE_BODY * n_body, n_chunks - 1):
        stage(1 - slot, c + 1)
        consume(slot, c, "single")
        slot = 1 - slot
    consume(slot, n_chunks - 1, "single")
    y = out_ref[...] + mod_ref[5:6, :] * acc_ref[...]
    if final:
        y = _rms(y) * fg_ref[...]
    out_ref[...] = y


def _ffn(h, mod, ctx_row, w_up, conv_w, conv_b, w_down, final_g, tm, mixer=None):
    b, l, d = h.shape
    n_chunks = w_down.shape[0]
    nh = l // HALO
    per = tm // HALO
    final = final_g is not None
    kern = functools.partial(_ffn_kernel, tm=tm, n_chunks=n_chunks, final=final, mixer=mixer is not None)
    tile_specs = [pl.BlockSpec((None, tm, d), lambda b, i: (b, i, 0)),
                  pl.BlockSpec((None, HALO, d), lambda b, i: (b, jnp.maximum(i * per - 1, 0), 0)),
                  pl.BlockSpec((None, HALO, d), lambda b, i: (b, jnp.minimum((i + 1) * per, nh - 1), 0))]
    in_specs = list(tile_specs)
    args = [h, h, h]
    if mixer is not None:
        o, w_proj = mixer
        in_specs += tile_specs + [_resident(w_proj.shape)]
        args += [o, o, o, w_proj]
    in_specs += [_mod_spec(d, ctx_row),
                 _resident(w_up.shape), _resident(conv_w.shape), _resident(conv_b.shape), _resident(w_down.shape)]
    args += [mod, w_up, conv_w, conv_b, w_down]
    if final:
        in_specs.append(_resident(final_g.shape))
        args.append(final_g)
    return pl.pallas_call(
        kern,
        grid=(b, l // tm),
        in_specs=in_specs,
        out_specs=pl.BlockSpec((None, tm, d), lambda b, i: (b, i, 0)),
        out_shape=jax.ShapeDtypeStruct(h.shape, F32),
        scratch_shapes=[pltpu.VMEM((tm + 2 * HALO, d), BF16), pltpu.VMEM((tm, d), F32),
                        pltpu.VMEM((2, 2, tm + 2 * HALO, w_up.shape[3]), F32),
                        pltpu.VMEM((tm, w_up.shape[3]), BF16)],
        compiler_params=_params(2),
        name="ffn",
    )(*args)


def _qkv_kernel(*refs, heads, with_q):
    if with_q:
        h_ref, mod_ref, w_ref, cos_ref, sa_ref, sb_ref, qt_ref, k_ref, vt_ref = refs
    else:
        h_ref, mod_ref, w_ref, k_ref, vt_ref = refs
    xl = _modulate(h_ref[...], mod_ref, 0).astype(BF16)

    def rope(yb):
        return (yb * cos_ref[...] + pltpu.roll(yb, LANES - SUB_HEAD // 4, 1) * sa_ref[...]
                + pltpu.roll(yb, SUB_HEAD // 4, 1) * sb_ref[...])

    group = 4
    for part in range(3 if with_q else 2):
        kind = part if with_q else part + 1
        for h0 in range(0, heads, group):
            c0 = (part * heads + h0) * HEAD_W
            y = jnp.dot(xl, w_ref[:, c0:c0 + group * HEAD_W], preferred_element_type=F32)
            for hh in range(group):
                h = h0 + hh
                yb = y[:, hh * HEAD_W:(hh + 1) * HEAD_W]
                if kind == 0:
                    qt_ref[h] = (rope(yb) * (SUB_HEAD ** -0.5 * math.log2(math.e))).T.astype(BF16)
                elif kind == 1:
                    k_ref[h] = (rope(yb) if with_q else yb).astype(BF16)
                else:
                    vt_ref[h, 0:HEAD_W, :] = yb.T.astype(BF16)
                    vt_ref[h, HEAD_W:VT_ROWS, :] = jnp.ones((VT_ROWS - HEAD_W, yb.shape[0]), BF16)


def _qkv(h, mod, ctx_row, w, tables, tm):
    b, l, d = h.shape
    with_q = tables is not None
    heads = w.shape[1] // HEAD_W // (3 if with_q else 2)
    kern = functools.partial(_qkv_kernel, heads=heads, with_q=with_q)
    in_specs = [pl.BlockSpec((None, tm, d), lambda b, i: (b, i, 0)), _mod_spec(d, ctx_row), _resident(w.shape)]
    args = [h, mod, w]
    out_specs = [pl.BlockSpec((None, heads, tm, HEAD_W), lambda b, i: (b, 0, i, 0)),
                 pl.BlockSpec((None, heads, None, VT_ROWS, tm), lambda b, i: (b, 0, i, 0, 0))]
    out_shape = [jax.ShapeDtypeStruct((b, heads, l, HEAD_W), BF16),
                 jax.ShapeDtypeStruct((b, heads, l // tm, VT_ROWS, tm), BF16)]
    if with_q:
        in_specs += [pl.BlockSpec((tm, LANES), lambda b, i: (i, 0))] * 3
        args += list(tables)
        out_specs.insert(0, pl.BlockSpec((None, heads, HEAD_W, tm), lambda b, i: (b, 0, 0, i)))
        out_shape.insert(0, jax.ShapeDtypeStruct((b, heads, HEAD_W, l), BF16))
    outs = pl.pallas_call(
        kern,
        grid=(b, l // tm),
        in_specs=in_specs,
        out_specs=out_specs,
        out_shape=out_shape,
        compiler_params=_params(2),
        name="qkv",
    )(*args)
    return outs if with_q else (None, *outs)


def _rope_tables(n_lat):
    rows = n_lat // GRID_W
    row_pos = jnp.broadcast_to(jnp.arange(rows, dtype=F32)[:, None], (rows, GRID_W)).reshape(-1)
    col_pos = jnp.broadcast_to(jnp.arange(GRID_W, dtype=F32)[None, :], (rows, GRID_W)).reshape(-1)
    n_freq = SUB_HEAD // 4
    inv_freq = ROPE_THETA ** (-jnp.arange(n_freq, dtype=F32) / n_freq)
    ang_r = row_pos[:, None] * inv_freq
    ang_c = col_pos[:, None] * inv_freq
    ang = jnp.concatenate([ang_r, ang_r, ang_c, ang_c] * 2, axis=-1)
    first = (jnp.arange(LANES) % (2 * n_freq)) < n_freq
    sin = jnp.sin(ang)
    return jnp.cos(ang), jnp.where(first, -sin, 0.0), jnp.where(first, 0.0, sin)


def _attn_kernel(qt_ref, kc_ref, vtc_ref, kl_ref, vtl_ref, lam_ref, g_ref, o_ref,
                 qs_ref, m_ref, acc_ref, grow_ref, *, tq, lambda_init):
    dim = lax.broadcasted_iota(jnp.int32, (HEAD_W, tq), 0)
    qt = qt_ref[...]
    zero = jnp.zeros_like(qt)
    qs_ref[:, 0:tq] = jnp.where(dim < SUB_HEAD, qt, zero)
    qs_ref[:, tq:2 * tq] = jnp.where(dim >= SUB_HEAD, qt, zero)
    n_chunks, _, tk = vtl_ref.shape

    def lat_k(j):
        return kl_ref[pl.ds(pl.multiple_of(j * tk, tk), tk), :]

    def finish():
        lp = lam_ref[...]
        lam = (jnp.exp(jnp.sum(lp[0:1, :] * lp[1:2, :], axis=1, keepdims=True))
               - jnp.exp(jnp.sum(lp[2:3, :] * lp[3:4, :], axis=1, keepdims=True)) + lambda_init)
        ot = acc_ref[0:HEAD_W, :] / acc_ref[HEAD_W:HEAD_W + 1, :]
        od = ot[:, 0:tq] - lam * ot[:, tq:2 * tq]
        od = od * lax.rsqrt(jnp.mean(od * od, axis=0, keepdims=True) + EPS) * (g_ref[...] * (1.0 - lambda_init))
        o_ref[...] = od.T.astype(BF16)

    def scores(k):
        return jnp.dot(k, qs_ref[...], preferred_element_type=F32)

    def single_pass_block(j, st=None):
        if st is None:
            st = scores(lat_k(j))
        m_prev = m_ref[...]
        pt = jnp.exp2(st - m_prev).astype(BF16)
        mx = jnp.max(st, axis=0, keepdims=True)
        m_new = jnp.maximum(m_prev, mx)
        grow_ref[...] = jnp.maximum(grow_ref[...], mx - m_prev)
        pv = jnp.dot(vtl_ref[j], pt, preferred_element_type=F32)
        acc_ref[...] = (acc_ref[...] + pv) * jnp.exp2(m_prev - m_new)
        m_ref[...] = m_new

    n_body = n_chunks // PIPE_BODY
    st = scores(kc_ref[...])
    st_first = scores(lat_k(0)) if n_body == 0 else None
    m_ref[...] = jnp.max(st, axis=0, keepdims=True)
    acc_ref[...] = jnp.dot(vtc_ref[0], jnp.exp2(st - m_ref[...]).astype(BF16), preferred_element_type=F32)
    grow_ref[...] = jnp.zeros_like(grow_ref)

    def single_pass_body(jj, carry):
        for t in range(PIPE_BODY):
            single_pass_block(PIPE_BODY * jj + t)
        return carry

    lax.fori_loop(0, n_body, single_pass_body, 0)
    for j in range(PIPE_BODY * n_body, n_chunks):
        single_pass_block(j, st_first if j == 0 else None)
    finish()

    @pl.when(jnp.max(grow_ref[...]) > MAX_GROWTH_LOG2)
    def _():
        m_ref[...] = jnp.full_like(m_ref, -jnp.inf)
        acc_ref[...] = jnp.zeros_like(acc_ref)

        def two_phase_block(k, vt):
            st = jnp.dot(k, qs_ref[...], preferred_element_type=F32)
            m_prev = m_ref[...]
            m_new = jnp.maximum(m_prev, jnp.max(st, axis=0, keepdims=True))
            pt = jnp.exp2(st - m_new).astype(BF16)
            acc_ref[...] = (jnp.exp2(m_prev - m_new) * acc_ref[...]
                            + jnp.dot(vt, pt, preferred_element_type=F32))
            m_ref[...] = m_new

        two_phase_block(kc_ref[...], vtc_ref[0])

        def two_phase_body(j, carry):
            two_phase_block(lat_k(j), vtl_ref[j])
            return carry

        lax.fori_loop(0, n_chunks, two_phase_body, 0)
        finish()


def _attn(qt, k_ctx, vt_ctx, k_lat, vt_lat, lam_params, subln_g, lambda_init, tq):
    b, heads, _, n_lat = qt.shape
    n_ctx = k_ctx.shape[2]
    kern = functools.partial(_attn_kernel, tq=tq, lambda_init=lambda_init)
    vt_block = lambda a: pl.BlockSpec((None, None) + a.shape[2:], lambda b, h, i: (b, h, 0, 0, 0))
    return pl.pallas_call(
        kern,
        grid=(b, heads, n_lat // tq),
        in_specs=[pl.BlockSpec((None, None, HEAD_W, tq), lambda b, h, i: (b, h, 0, i)),
                  pl.BlockSpec((None, None, n_ctx, HEAD_W), lambda b, h, i: (b, h, 0, 0)),
                  vt_block(vt_ctx),
                  pl.BlockSpec((None, None, n_lat, HEAD_W), lambda b, h, i: (b, h, 0, 0)),
                  vt_block(vt_lat),
                  pl.BlockSpec(lam_params.shape, lambda b, h, i: (0, 0)),
                  pl.BlockSpec(subln_g.shape, lambda b, h, i: (0, 0))],
        out_specs=pl.BlockSpec((None, tq, HEAD_W), lambda b, h, i: (b, i, h)),
        out_shape=jax.ShapeDtypeStruct((b, n_lat, heads * HEAD_W), BF16),
        scratch_shapes=[pltpu.VMEM((HEAD_W, 2 * tq), BF16), pltpu.VMEM((1, 2 * tq), F32),
                        pltpu.VMEM((VT_ROWS, 2 * tq), F32), pltpu.VMEM((1, 2 * tq), F32)],
        compiler_params=_params(3),
        name="attn",
    )(qt, k_ctx, vt_ctx, k_lat, vt_lat, lam_params, subln_g)


def _tile(l, target):
    return min(l, target)


def kernel(x, c, ctx, c_ctx, ada_w, ada_b, gm_w_in, gm_norm_g, gm_w_s, gm_b_s, gm_w_out, da_w_qkv, da_lambda_q1, da_lambda_k1, da_lambda_q2, da_lambda_k2, da_subln_g, da_w_out, ffn_w_up, ffn_conv_w, ffn_conv_b, ffn_w_down, final_norm_g):
    bsz, n_lat, d = x.shape
    depth = ada_w.shape[0]
    ffn_dim = ffn_w_down.shape[1]
    assert depth == 2 and bsz < ADA_ROWS and n_lat % GRID_W == 0
    ctx_row = bsz
    ffn_cw = 2 * LANES
    n_chunks = ffn_dim // ffn_cw
    assert n_chunks * ffn_cw == ffn_dim

    cond = jnp.zeros((ADA_ROWS, d), F32).at[:bsz].set(c).at[ctx_row].set(c_ctx)
    mod = _ada(cond, ada_w, ada_b).reshape(depth, ADA_ROWS, 6, d)

    def ffn_weights(i):
        w_up = ffn_w_up[i].astype(BF16).reshape(d, 2, n_chunks, ffn_cw).transpose(1, 2, 0, 3)
        cw = ffn_conv_w[i].reshape(CONV_W, 2, n_chunks, ffn_cw).transpose(1, 2, 0, 3)
        cb = ffn_conv_b[i].reshape(2, n_chunks, 1, ffn_cw)
        w_dn = ffn_w_down[i].astype(BF16).reshape(n_chunks, ffn_cw, d)
        return w_up, cw, cb, w_dn

    tm_lat = _tile(n_lat, TOKEN_TILE)
    tm_ctx = _tile(ctx.shape[1], TOKEN_TILE)

    gm = (gm_w_in[0].astype(BF16), gm_norm_g[0].reshape(1, -1), gm_w_s[0].astype(BF16),
          gm_b_s[0][:, :, None], gm_w_out[0].astype(BF16))
    fw = ffn_weights(0)
    h = _gmlp(x, mod[0], None, *gm, tm_lat)
    hc = _gmlp(ctx, mod[0], ctx_row, *gm, tm_ctx)
    h = _ffn(h, mod[0], None, *fw, None, tm_lat)
    hc = _ffn(hc, mod[0], ctx_row, *fw, None, tm_ctx)

    lambda_init = 0.8 - 0.6 * math.exp(-0.3 * 1)
    w_qkv = da_w_qkv[0].astype(BF16)
    da_width = w_qkv.shape[1] // 3
    qt, k_lat, vt_lat = _qkv(h, mod[1], None, w_qkv, _rope_tables(n_lat), tm_lat)
    _, k_ctx, vt_ctx = _qkv(hc, mod[1], ctx_row, w_qkv[:, da_width:], None, tm_ctx)
    lam_params = jnp.stack([da_lambda_q1[0], da_lambda_k1[0], da_lambda_q2[0], da_lambda_k2[0]])
    o = _attn(qt, k_ctx, vt_ctx, k_lat, vt_lat, lam_params, da_subln_g[0].reshape(-1, 1), lambda_init,
              _tile(n_lat, QUERY_TILE))
    fw = ffn_weights(1)
    return _ffn(h, mod[1], None, *fw, final_norm_g.reshape(1, -1), tm_lat, mixer=(o, da_w_out[0].astype(BF16)))
```
